```python
import math
import jax, jax.numpy as jnp
from jax import lax
import numpy as np

D_MODEL = 4096
BATCH = 2
SEQ = 8192
DEPTH = 2

HEAD_DIM = 128
A_HEADS = (D_MODEL // 2) // HEAD_DIM
A_WIDTH = A_HEADS * HEAD_DIM
B_HEADS = (D_MODEL // 4) // HEAD_DIM
B_WIDTH = B_HEADS * HEAD_DIM
POOL_WINDOWS = (2, 4, 8, 16)
C_GROUPS = len(POOL_WINDOWS)
C_WIDTH = D_MODEL - A_WIDTH - B_WIDTH
C_GROUP_DIM = C_WIDTH // C_GROUPS
MIX_WIDTH = A_WIDTH + B_WIDTH + C_WIDTH
Q_BLOCK = 128
CHUNK = 128
IN_COLS = 4 * A_WIDTH + 3 * B_WIDTH + 2 * C_WIDTH
EPS = 1e-6

kernel_name = "hybrid_stickbreak_sgu_pool_block"


def rmsnorm(x, g):
    xf = x.astype(jnp.float32)
    y = xf * lax.rsqrt(jnp.mean(xf * xf, axis=-1, keepdims=True) + EPS)
    return (y * g.astype(jnp.float32)).astype(x.dtype)


def stick_breaking_attention(q, k, v):
    b, s, h, dh = q.shape
    nb = s // Q_BLOCK
    scale = 1.0 / math.sqrt(dh)
    kt = jnp.transpose(k, (0, 2, 1, 3)).astype(jnp.float32)
    vt = jnp.transpose(v, (0, 2, 1, 3))
    qb = q.reshape(b, nb, Q_BLOCK, h, dh).transpose(1, 0, 3, 2, 4)
    starts = jnp.arange(nb, dtype=jnp.int32) * Q_BLOCK
    key_pos = jnp.arange(s, dtype=jnp.int32)

    def block(args):
        qblk, t0 = args
        z = jnp.einsum('bhqd,bhkd->bhqk', qblk.astype(jnp.float32), kt) * scale
        q_pos = t0 + jnp.arange(Q_BLOCK, dtype=jnp.int32)
        causal = key_pos[None, :] < q_pos[:, None]
        log_beta = jax.nn.log_sigmoid(z)
        log_1m = jnp.where(causal, jax.nn.log_sigmoid(-z), 0.0)
        suffix = lax.cumsum(log_1m, axis=3, reverse=True) - log_1m
        attn = jnp.where(causal, jnp.exp(log_beta + suffix), 0.0)
        return jnp.einsum('bhqk,bhkd->bhqd', attn.astype(vt.dtype), vt)

    out = lax.map(block, (qb, starts))
    return out.transpose(1, 0, 3, 2, 4).reshape(b, s, h * dh)


def chunked_spatial_gating(u, v, sgu_g, w_s, b_s):
    b, s, _ = v.shape
    nc = s // CHUNK
    vh = rmsnorm(v.reshape(b, s, B_HEADS, HEAD_DIM), sgu_g.reshape(B_HEADS, HEAD_DIM))
    vh = vh.reshape(b, nc, CHUNK, B_HEADS, HEAD_DIM)
    tri = jnp.tril(jnp.ones((CHUNK, CHUNK), dtype=w_s.dtype))
    w_masked = w_s * tri[None]
    mixed = jnp.einsum('hts,bnshd->bnthd', w_masked, vh)
    mixed = mixed + jnp.transpose(b_s)[None, None, :, :, None]
    return u * mixed.reshape(b, s, B_WIDTH)


def causal_pool_minus_self(xg, window):
    b, s, c = xg.shape
    csum = jnp.cumsum(xg.astype(jnp.float32), axis=1)
    shifted = jnp.concatenate([jnp.zeros((b, window, c), jnp.float32), csum[:, :-window]], axis=1)
    count = jnp.minimum(jnp.arange(1, s + 1, dtype=jnp.int32), window).astype(jnp.float32)
    mean = (csum - shifted) / count[None, :, None]
    return (mean - xg.astype(jnp.float32)).astype(xg.dtype)


def multiscale_pool(xc, w_pool, pool_scale):
    b, s, _ = xc.shape
    groups = xc.reshape(b, s, C_GROUPS, C_GROUP_DIM)
    pooled = jnp.stack([causal_pool_minus_self(groups[:, :, g], w)
                        for g, w in enumerate(POOL_WINDOWS)], axis=2)
    mapped = jnp.einsum('bsgc,gcd->bsgd', pooled, w_pool)
    return mapped.reshape(b, s, C_WIDTH) * pool_scale


def hybrid_layer(x, norm_g, w_in, q_norm_g, k_norm_g, sgu_norm_g,
                 w_spatial, b_spatial, w_pool, pool_scale, w_out):
    b, s, _ = x.shape
    h = rmsnorm(x, norm_g)
    proj = jnp.einsum('bsd,dc->bsc', h, w_in)
    splits = np.cumsum([A_WIDTH] * 4 + [B_WIDTH] * 3 + [C_WIDTH])
    qa, ka, va, ga, ub, vb, gb, xc, gc = jnp.split(proj, splits.tolist(), axis=-1)

    qa = rmsnorm(qa.reshape(b, s, A_HEADS, HEAD_DIM), q_norm_g)
    ka = rmsnorm(ka.reshape(b, s, A_HEADS, HEAD_DIM), k_norm_g)
    va = va.reshape(b, s, A_HEADS, HEAD_DIM)
    ya = stick_breaking_attention(qa, ka, va)

    yb = chunked_spatial_gating(ub, vb, sgu_norm_g, w_spatial, b_spatial)

    yc = multiscale_pool(xc, w_pool, pool_scale)

    y = jnp.concatenate([ya * jax.nn.silu(ga),
                         yb * jax.nn.silu(gb),
                         yc * jax.nn.silu(gc)], axis=-1)
    return x + jnp.einsum('bsc,cd->bsd', y, w_out)


def setup_inputs(seed: int = 0) -> dict:
    key = jax.random.key(seed)
    ks = jax.random.split(key, 12)
    f32 = jnp.float32
    x = jax.random.normal(ks[0], (BATCH, SEQ, D_MODEL), f32)
    norm_g = 1.0 + 0.02 * jax.random.normal(ks[1], (DEPTH, D_MODEL), f32)
    w_in = jax.random.normal(ks[2], (DEPTH, D_MODEL, IN_COLS), f32) * D_MODEL ** -0.5
    q_norm_g = 1.0 + 0.02 * jax.random.normal(ks[3], (DEPTH, HEAD_DIM), f32)
    k_norm_g = 1.0 + 0.02 * jax.random.normal(ks[4], (DEPTH, HEAD_DIM), f32)
    sgu_norm_g = 1.0 + 0.02 * jax.random.normal(ks[5], (DEPTH, B_WIDTH), f32)
    w_spatial = jax.random.normal(ks[6], (DEPTH, B_HEADS, CHUNK, CHUNK), f32) * (0.5 * CHUNK ** -0.5)
    b_spatial = 1.0 + 0.02 * jax.random.normal(ks[7], (DEPTH, B_HEADS, CHUNK), f32)
    w_pool = jax.random.normal(ks[8], (DEPTH, C_GROUPS, C_GROUP_DIM, C_GROUP_DIM), f32) * C_GROUP_DIM ** -0.5
    pool_scale = 1.0 + 0.02 * jax.random.normal(ks[9], (DEPTH, C_WIDTH), f32)
    w_out = jax.random.normal(ks[10], (DEPTH, MIX_WIDTH, D_MODEL), f32) * MIX_WIDTH ** -0.5
    return {"x": x, "norm_g": norm_g, "w_in": w_in, "q_norm_g": q_norm_g,
            "k_norm_g": k_norm_g, "sgu_norm_g": sgu_norm_g, "w_spatial": w_spatial,
            "b_spatial": b_spatial, "w_pool": w_pool, "pool_scale": pool_scale,
            "w_out": w_out}


def reference(x, norm_g, w_in, q_norm_g, k_norm_g, sgu_norm_g,
              w_spatial, b_spatial, w_pool, pool_scale, w_out):
    for layer in range(DEPTH):
        x = hybrid_layer(x, norm_g[layer], w_in[layer], q_norm_g[layer], k_norm_g[layer],
                         sgu_norm_g[layer], w_spatial[layer], b_spatial[layer],
                         w_pool[layer], pool_scale[layer], w_out[layer])
    return x
```

```python
import functools
import math

import jax
import jax.numpy as jnp
from jax import lax
from jax.experimental import pallas as pl
from jax.experimental.pallas import tpu as pltpu

F32 = jnp.float32
BF16 = jnp.bfloat16

LANES = 128
HEAD_DIM = 128
POOL_WINDOWS = (2, 4, 8, 16)
CHUNK = 128
EPS = 1e-6
VMEM_LIMIT_BYTES = 56 * 1024 * 1024

EXP_UNDERFLOW = -104.0


def _silu(g):
    return g / (1.0 + jnp.exp(-g))


def _rms_scale(x):
    return lax.rsqrt(jnp.mean(x * x, axis=-1, keepdims=True) + EPS)


def _inproj_kernel(x_ref, g_ref, w_ref, o_ref, hn_ref, *, tm, tn, rows_per_step):
    j = pl.program_id(1)

    @pl.when(j == 0)
    def _():
        g = g_ref[...]

        def norm_rows(r, _):
            sl = pl.ds(pl.multiple_of(r * rows_per_step, rows_per_step), rows_per_step)
            x = x_ref[sl, :]
            hn_ref[sl, :] = (x * _rms_scale(x) * g).astype(BF16)
            return 0

        lax.fori_loop(0, tm // rows_per_step, norm_rows, 0)

    acc = jnp.dot(hn_ref[...], w_ref[...], preferred_element_type=F32)
    for c in range(tn // LANES):
        o_ref[c] = acc[:, c * LANES:(c + 1) * LANES].astype(BF16)


def _inproj(x2d, g, w_bf16, *, tm, tn):
    m, d = x2d.shape
    n = w_bf16.shape[1]
    kern = functools.partial(_inproj_kernel, tm=tm, tn=tn, rows_per_step=32)
    return pl.pallas_call(
        kern,
        out_shape=jax.ShapeDtypeStruct((n // LANES, m, LANES), BF16),
        grid=(m // tm, n // tn),
        in_specs=[
            pl.BlockSpec((tm, d), lambda i, j: (i, 0)),
            pl.BlockSpec((1, d), lambda i, j: (0, 0)),
            pl.BlockSpec((d, tn), lambda i, j: (0, j)),
        ],
        out_specs=pl.BlockSpec((tn // LANES, tm, LANES), lambda i, j: (j, i, 0)),
        scratch_shapes=[pltpu.VMEM((tm, d), BF16)],
        compiler_params=pltpu.CompilerParams(
            dimension_semantics=("parallel", "arbitrary"),
            vmem_limit_bytes=VMEM_LIMIT_BYTES),
        name="inproj",
    )(x2d, g.reshape(1, d), w_bf16)


def _attn_kernel(q_ref, k_ref, v_ref, gate_ref, qg_ref, kg_ref, o_ref,
                 kn_ref, tri_ref, *, tq, seq, rows_per_step):
    qi = pl.program_id(2)
    tk = tq

    @pl.when(qi == 0)
    def _():
        kg = kg_ref[...]

        def norm_rows(r, _):
            sl = pl.ds(pl.multiple_of(r * rows_per_step, rows_per_step), rows_per_step)
            k = k_ref[sl, :].astype(F32)
            kn_ref[sl, :] = (k * _rms_scale(k) * kg).astype(BF16)
            return 0

        lax.fori_loop(0, seq // rows_per_step, norm_rows, 0)
        row = lax.broadcasted_iota(jnp.int32, (tk, tk), 0)
        col = lax.broadcasted_iota(jnp.int32, (tk, tk), 1)
        tri_ref[...] = jnp.where(row > col, 1.0, 0.0).astype(BF16)

    q = q_ref[...].astype(F32)
    qn = (q * _rms_scale(q) * qg_ref[...] * (1.0 / math.sqrt(HEAD_DIM))).astype(BF16)
    tri = tri_ref[...]

    def block(kb, carry, acc, mask):
        ksl = pl.ds(pl.multiple_of(kb * tk, tk), tk)
        kblk = kn_ref[ksl, :]
        vblk = v_ref[ksl, :]
        z = lax.dot_general(qn, kblk, (((1,), (1,)), ((), ())),
                            preferred_element_type=F32)
        log_beta = jnp.minimum(z, 0.0) - jnp.log(1.0 + jnp.exp(-jnp.abs(z)))
        log_1m = log_beta - z
        if mask is not None:
            log_1m = jnp.where(mask, log_1m, 0.0)
        hi = log_1m.astype(BF16)
        lo = (log_1m - hi.astype(F32)).astype(BF16)
        suffix = (jnp.dot(hi, tri, preferred_element_type=F32)
                  + jnp.dot(lo, tri, preferred_element_type=F32))
        attn = jnp.exp(log_beta + suffix + carry)
        if mask is not None:
            attn = jnp.where(mask, attn, 0.0)
        acc = acc + jnp.dot(attn.astype(BF16), vblk, preferred_element_type=F32)
        carry = carry + (suffix[:, :1] + log_1m[:, :1])
        return carry, acc

    row = lax.broadcasted_iota(jnp.int32, (tq, tk), 0)
    col = lax.broadcasted_iota(jnp.int32, (tq, tk), 1)
    carry0 = jnp.zeros((tq, 1), F32)
    acc0 = jnp.zeros((tq, HEAD_DIM), F32)
    carry, acc = block(qi, carry0, acc0, col < row)

    def body(i, state):
        return block(qi - 1 - i, state[0], state[1], None)

    carry, acc = lax.fori_loop(0, qi, body, (carry, acc))
    o_ref[...] = (acc * _silu(gate_ref[...].astype(F32))).astype(BF16)


def _attention(proj4, q_g, k_g, *, heads, tq):
    _, b, s, _ = proj4.shape
    kern = functools.partial(_attn_kernel, tq=tq, seq=s, rows_per_step=256)
    blk = lambda off: pl.BlockSpec((None, None, tq, LANES),
                                   lambda bi, h, qi: (off + h, bi, qi, 0))
    full = lambda off: pl.BlockSpec((None, None, s, LANES),
                                    lambda bi, h, qi: (off + h, bi, 0, 0))
    vec = pl.BlockSpec((1, LANES), lambda bi, h, qi: (0, 0))
    return pl.pallas_call(
        kern,
        out_shape=jax.ShapeDtypeStruct((heads, b, s, LANES), BF16),
        grid=(b, heads, s // tq),
        in_specs=[blk(0), full(heads), full(2 * heads), blk(3 * heads), vec, vec],
        out_specs=pl.BlockSpec((None, None, tq, LANES),
                               lambda bi, h, qi: (h, bi, qi, 0)),
        scratch_shapes=[pltpu.VMEM((s, LANES), BF16), pltpu.VMEM((tq, tq), BF16)],
        compiler_params=pltpu.CompilerParams(
            dimension_semantics=("parallel", "parallel", "arbitrary"),
            vmem_limit_bytes=VMEM_LIMIT_BYTES),
        name="stickbreak_attn",
    )(proj4, proj4, proj4, proj4, q_g.reshape(1, LANES), k_g.reshape(1, LANES))


def _sgu_kernel(u_ref, v_ref, gate_ref, g_ref, w_ref, b_ref, o_ref, *, heads, ts):
    row = lax.broadcasted_iota(jnp.int32, (CHUNK, CHUNK), 0)
    col = lax.broadcasted_iota(jnp.int32, (CHUNK, CHUNK), 1)
    for h in range(heads):
        w = jnp.where(col <= row, w_ref[h], 0.0).astype(BF16)
        bias = b_ref[h]
        g = g_ref[h]
        for c in range(ts // CHUNK):
            sl = slice(c * CHUNK, (c + 1) * CHUNK)
            v = v_ref[h, sl, :].astype(F32)
            vn = (v * _rms_scale(v) * g).astype(BF16)
            mixed = jnp.dot(w, vn, preferred_element_type=F32) + bias
            y = u_ref[h, sl, :].astype(F32) * mixed * _silu(gate_ref[h, sl, :].astype(F32))
            o_ref[h, sl, :] = y.astype(BF16)


def _sgu(proj3, sgu_g, w_s, b_s, *, heads, first_block, ts):
    _, m, _ = proj3.shape
    kern = functools.partial(_sgu_kernel, heads=heads, ts=ts)
    base = first_block // heads
    blk = lambda k: pl.BlockSpec((heads, ts, LANES), lambda i: (base + k, i, 0))
    b_bcast = jnp.broadcast_to(b_s[:, :, None], (heads, CHUNK, LANES))
    return pl.pallas_call(
        kern,
        out_shape=jax.ShapeDtypeStruct((heads, m, LANES), BF16),
        grid=(m // ts,),
        in_specs=[blk(0), blk(1), blk(2),
                  pl.BlockSpec((heads, 1, LANES), lambda i: (0, 0, 0)),
                  pl.BlockSpec((heads, CHUNK, CHUNK), lambda i: (0, 0, 0)),
                  pl.BlockSpec((heads, CHUNK, LANES), lambda i: (0, 0, 0))],
        out_specs=pl.BlockSpec((heads, ts, LANES), lambda i: (0, i, 0)),
        compiler_params=pltpu.CompilerParams(
            dimension_semantics=("parallel",),
            vmem_limit_bytes=VMEM_LIMIT_BYTES),
        name="spatial_gating",
    )(proj3, proj3, proj3, sgu_g.reshape(heads, 1, LANES), w_s, b_bcast)


def _pool_kernel(x_ref, halo_ref, gate_ref, w_ref, scale_ref, o_ref, *, ts, halo):
    i = pl.program_id(1)
    t = lax.broadcasted_iota(jnp.int32, (ts, ts), 0)
    j = lax.broadcasted_iota(jnp.int32, (ts, ts), 1)
    th = lax.broadcasted_iota(jnp.int32, (ts, halo), 0)
    jh = lax.broadcasted_iota(jnp.int32, (ts, halo), 1)
    pos = i * ts + lax.broadcasted_iota(jnp.int32, (ts, 1), 0)
    for g, window in enumerate(POOL_WINDOWS):
        halo_window = jnp.where(i > 0, window, 0)
        x = jnp.concatenate([x_ref[2 * g], x_ref[2 * g + 1]], axis=-1)
        xh = jnp.concatenate([halo_ref[2 * g], halo_ref[2 * g + 1]], axis=-1)
        band = jnp.where((j <= t) & (j > t - window), 1.0, 0.0).astype(BF16)
        band_h = jnp.where(jh - halo > th - halo_window, 1.0, 0.0).astype(BF16)
        wsum = (jnp.dot(band, x, preferred_element_type=F32)
                + jnp.dot(band_h, xh, preferred_element_type=F32))
        count = jnp.minimum(pos + 1, window).astype(F32)
        pooled = wsum / count - x.astype(F32)
        mapped = jnp.dot(pooled.astype(BF16), w_ref[g], preferred_element_type=F32)
        for half in range(2):
            c = 2 * g + half
            lanes = slice(half * LANES, (half + 1) * LANES)
            y = mapped[:, lanes] * scale_ref[c] * _silu(gate_ref[c].astype(F32))
            o_ref[c] = y.astype(BF16)


def _pool(proj3, w_pool_bf16, pool_scale, *, batch, seq, first_block, ts):
    _, m, _ = proj3.shape
    nblk = 2 * len(POOL_WINDOWS)
    halo = LANES
    assert max(POOL_WINDOWS) <= halo and ts % halo == 0
    kern = functools.partial(_pool_kernel, ts=ts, halo=halo)
    base = first_block // nblk
    steps = seq // ts
    blk = lambda k: pl.BlockSpec((nblk, ts, LANES),
                                 lambda b, i: (base + k, b * steps + i, 0))
    halo_spec = pl.BlockSpec(
        (nblk, halo, LANES),
        lambda b, i: (base, jnp.maximum((b * seq + i * ts) // halo - 1, 0), 0))
    return pl.pallas_call(
        kern,
        out_shape=jax.ShapeDtypeStruct((nblk, m, LANES), BF16),
        grid=(batch, steps),
        in_specs=[blk(0), halo_spec, blk(1),
                  pl.BlockSpec(w_pool_bf16.shape, lambda b, i: (0, 0, 0)),
                  pl.BlockSpec((nblk, 1, LANES), lambda b, i: (0, 0, 0))],
        out_specs=pl.BlockSpec((nblk, ts, LANES), lambda b, i: (0, b * steps + i, 0)),
        compiler_params=pltpu.CompilerParams(
            dimension_semantics=("parallel", "parallel"),
            vmem_limit_bytes=VMEM_LIMIT_BYTES),
        name="causal_pool",
    )(proj3, proj3, proj3, w_pool_bf16, pool_scale.reshape(nblk, 1, LANES))


def _outproj_kernel(ya_ref, yb_ref, yc_ref, w_ref, x_ref, o_ref, y_ref):
    j = pl.program_id(1)

    @pl.when(j == 0)
    def _():
        c = 0
        for ref in (ya_ref, yb_ref, yc_ref):
            for k in range(ref.shape[0]):
                y_ref[:, c * LANES:(c + 1) * LANES] = ref[k]
                c += 1

    o_ref[...] = x_ref[...] + jnp.dot(y_ref[...], w_ref[...], preferred_element_type=F32)


def _outproj(ya, yb, yc, w_bf16, x2d, *, tm, tn):
    m, d = x2d.shape
    kdim = w_bf16.shape[0]
    yspec = lambda a: pl.BlockSpec((a.shape[0], tm, LANES), lambda i, j: (0, i, 0))
    return pl.pallas_call(
        _outproj_kernel,
        out_shape=jax.ShapeDtypeStruct((m, d), F32),
        grid=(m // tm, d // tn),
        in_specs=[yspec(ya), yspec(yb), yspec(yc),
                  pl.BlockSpec((kdim, tn), lambda i, j: (0, j)),
                  pl.BlockSpec((tm, tn), lambda i, j: (i, j))],
        out_specs=pl.BlockSpec((tm, tn), lambda i, j: (i, j)),
        scratch_shapes=[pltpu.VMEM((tm, kdim), BF16)],
        compiler_params=pltpu.CompilerParams(
            dimension_semantics=("parallel", "arbitrary"),
            vmem_limit_bytes=VMEM_LIMIT_BYTES),
        name="outproj",
    )(ya, yb, yc, w_bf16, x2d)


def kernel(x, norm_g, w_in, q_norm_g, k_norm_g, sgu_norm_g, w_spatial, b_spatial,
           w_pool, pool_scale, w_out):
    batch, seq, d_model = x.shape
    depth = norm_g.shape[0]
    b_heads = w_spatial.shape[1]
    b_width = b_heads * HEAD_DIM
    c_width = pool_scale.shape[1]
    a_width = d_model - b_width - c_width
    a_heads = a_width // HEAD_DIM
    in_cols = w_in.shape[2]
    assert in_cols == 4 * a_width + 3 * b_width + 2 * c_width
    assert w_pool.shape[1:] == (len(POOL_WINDOWS), 2 * LANES, 2 * LANES)
    m = batch * seq

    w_in_bf16 = w_in.astype(BF16)
    w_out_bf16 = w_out.astype(BF16)
    w_pool_bf16 = w_pool.astype(BF16)

    x2d = x.reshape(m, d_model)
    for layer in range(depth):
        proj3 = _inproj(x2d, norm_g[layer], w_in_bf16[layer], tm=512, tn=1024)
        proj4 = proj3.reshape(in_cols // LANES, batch, seq, LANES)
        ya = _attention(proj4, q_norm_g[layer], k_norm_g[layer], heads=a_heads, tq=256)
        yb = _sgu(proj3, sgu_norm_g[layer], w_spatial[layer], b_spatial[layer],
                  heads=b_heads, first_block=4 * a_heads, ts=512)
        yc = _pool(proj3, w_pool_bf16[layer], pool_scale[layer], batch=batch, seq=seq,
                   first_block=4 * a_heads + 3 * b_heads, ts=256)
        x2d = _outproj(ya.reshape(a_heads, m, LANES), yb, yc, w_out_bf16[layer], x2d,
                       tm=512, tn=1024)
    return x2d.reshape(batch, seq, d_model)
```

```python
import functools
import math

import jax
import jax.numpy as jnp
from jax import lax
from jax.experimental import pallas as pl
from jax.experimental.pallas import tpu as pltpu

F32 = jnp.float32
BF16 = jnp.bfloat16

LANES = 128
HEAD_DIM = 128
POOL_WINDOWS = (2, 4, 8, 16)
CHUNK = 128
EPS = 1e-6
VMEM_LIMIT_BYTES = 56 * 1024 * 1024

EXP_UNDERFLOW = -104.0


def _silu(g):
    return g / (1.0 + jnp.exp(-g))


def _rms_scale(x):
    return lax.rsqrt(jnp.mean(x * x, axis=-1, keepdims=True) + EPS)


def _inproj_kernel(x_ref, g_ref, w_ref, o_ref, hn_ref, *, tm, tn, rows_per_step):
    j = pl.program_id(1)

    @pl.when(j == 0)
    def _():
        g = g_ref[...]

        def norm_rows(r, _):
            sl = pl.ds(pl.multiple_of(r * rows_per_step, rows_per_step), rows_per_step)
            x = x_ref[sl, :]
            hn_ref[sl, :] = (x * _rms_scale(x) * g).astype(BF16)
            return 0

        lax.fori_loop(0, tm // rows_per_step, norm_rows, 0)

    acc = jnp.dot(hn_ref[...], w_ref[...], preferred_element_type=F32)
    for c in range(tn // LANES):
        o_ref[c] = acc[:, c * LANES:(c + 1) * LANES].astype(BF16)


def _inproj(x2d, g, w_bf16, layer, *, tm, tn):
    m, d = x2d.shape
    n = w_bf16.shape[2]
    kern = functools.partial(_inproj_kernel, tm=tm, tn=tn, rows_per_step=32)
    return pl.pallas_call(
        kern,
        out_shape=jax.ShapeDtypeStruct((n // LANES, m, LANES), BF16),
        grid=(m // tm, n // tn),
        in_specs=[
            pl.BlockSpec((tm, d), lambda i, j: (i, 0)),
            pl.BlockSpec((1, d), lambda i, j: (0, 0)),
            pl.BlockSpec((None, d, tn), lambda i, j: (layer, 0, j)),
        ],
        out_specs=pl.BlockSpec((tn // LANES, tm, LANES), lambda i, j: (j, i, 0)),
        scratch_shapes=[pltpu.VMEM((tm, d), BF16)],
        compiler_params=pltpu.CompilerParams(
            dimension_semantics=("parallel", "arbitrary"),
            vmem_limit_bytes=VMEM_LIMIT_BYTES),
        name="inproj",
    )(x2d, g.reshape(1, d), w_bf16)


def _attn_kernel(q_ref, k_ref, v_ref, gate_ref, qg_ref, kg_ref, o_ref,
                 kn_ref, tri_ref, *, tq, seq, rows_per_step):
    qi = pl.program_id(2)
    tk = tq

    @pl.when(qi == 0)
    def _():
        kg = kg_ref[...]

        def norm_rows(r, _):
            sl = pl.ds(pl.multiple_of(r * rows_per_step, rows_per_step), rows_per_step)
            k = k_ref[sl, :].astype(F32)
            kn_ref[sl, :] = (k * _rms_scale(k) * kg).astype(BF16)
            return 0

        lax.fori_loop(0, seq // rows_per_step, norm_rows, 0)
        row = lax.broadcasted_iota(jnp.int32, (tk, tk), 0)
        col = lax.broadcasted_iota(jnp.int32, (tk, tk), 1)
        tri_ref[...] = jnp.where(row > col, 1.0, 0.0).astype(BF16)

    q = q_ref[...].astype(F32)
    qn = (q * _rms_scale(q) * qg_ref[...] * (1.0 / math.sqrt(HEAD_DIM))).astype(BF16)
    tri = tri_ref[...]

    def block(kb, carry, acc, mask, v_scale=None):
        ksl = pl.ds(pl.multiple_of(kb * tk, tk), tk)
        kblk = kn_ref[ksl, :]
        vblk = v_ref[ksl, :]
        if v_scale is not None:
            vblk = vblk * v_scale
        z = lax.dot_general(qn, kblk, (((1,), (1,)), ((), ())),
                            preferred_element_type=F32)
        log_beta = jnp.minimum(z, 0.0) - jnp.log(1.0 + jnp.exp(-jnp.abs(z)))
        log_1m = log_beta - z
        if mask is not None:
            log_1m = jnp.where(mask, log_1m, 0.0)
        hi = log_1m.astype(BF16)
        lo = (log_1m - hi.astype(F32)).astype(BF16)
        suffix = (jnp.dot(hi, tri, preferred_element_type=F32)
                  + jnp.dot(lo, tri, preferred_element_type=F32))
        attn = jnp.exp(log_beta + suffix + carry)
        if mask is not None:
            attn = jnp.where(mask, attn, 0.0)
        acc = acc + jnp.dot(attn.astype(BF16), vblk, preferred_element_type=F32)
        carry = carry + (suffix[:, :1] + log_1m[:, :1])
        return carry, acc

    row = lax.broadcasted_iota(jnp.int32, (tq, tk), 0)
    col = lax.broadcasted_iota(jnp.int32, (tq, tk), 1)
    carry0 = jnp.zeros((tq, 1), F32)
    acc0 = jnp.zeros((tq, HEAD_DIM), F32)
    carry, acc = block(qi, carry0, acc0, col < row)
    has_prev = (qi > 0).astype(F32).astype(BF16)
    carry, acc = block(jnp.maximum(qi - 1, 0), carry, acc, None, v_scale=has_prev)

    def live(state):
        kb, carry, _ = state
        return jnp.logical_and(kb >= 0, jnp.max(carry) > EXP_UNDERFLOW)

    def body(state):
        kb, carry, acc = state
        carry, acc = block(kb, carry, acc, None)
        return kb - 1, carry, acc

    _, carry, acc = lax.while_loop(live, body, (qi - 2, carry, acc))
    o_ref[...] = (acc * _silu(gate_ref[...].astype(F32))).astype(BF16)


def _attention(proj4, q_g, k_g, *, heads, tq):
    _, b, s, _ = proj4.shape
    kern = functools.partial(_attn_kernel, tq=tq, seq=s, rows_per_step=256)
    blk = lambda off: pl.BlockSpec((None, None, tq, LANES),
                                   lambda bi, h, qi: (off + h, bi, qi, 0))
    full = lambda off: pl.BlockSpec((None, None, s, LANES),
                                    lambda bi, h, qi: (off + h, bi, 0, 0))
    vec = pl.BlockSpec((1, LANES), lambda bi, h, qi: (0, 0))
    return pl.pallas_call(
        kern,
        out_shape=jax.ShapeDtypeStruct((heads, b, s, LANES), BF16),
        grid=(b, heads, s // tq),
        in_specs=[blk(0), full(heads), full(2 * heads), blk(3 * heads), vec, vec],
        out_specs=pl.BlockSpec((None, None, tq, LANES),
                               lambda bi, h, qi: (h, bi, qi, 0)),
        scratch_shapes=[pltpu.VMEM((s, LANES), BF16), pltpu.VMEM((tq, tq), BF16)],
        compiler_params=pltpu.CompilerParams(
            dimension_semantics=("parallel", "parallel", "arbitrary"),
            vmem_limit_bytes=VMEM_LIMIT_BYTES),
        name="stickbreak_attn",
    )(proj4, proj4, proj4, proj4, q_g.reshape(1, LANES), k_g.reshape(1, LANES))


def _sgu_kernel(u_ref, v_ref, gate_ref, g_ref, w_ref, b_ref, o_ref, *, heads, ts):
    row = lax.broadcasted_iota(jnp.int32, (CHUNK, CHUNK), 0)
    col = lax.broadcasted_iota(jnp.int32, (CHUNK, CHUNK), 1)
    for h in range(heads):
        w = jnp.where(col <= row, w_ref[h], 0.0).astype(BF16)
        bias = b_ref[h]
        g = g_ref[h]
        for c in range(ts // CHUNK):
            sl = slice(c * CHUNK, (c + 1) * CHUNK)
            v = v_ref[h, sl, :].astype(F32)
            vn = (v * _rms_scale(v) * g).astype(BF16)
            mixed = jnp.dot(w, vn, preferred_element_type=F32) + bias
            y = u_ref[h, sl, :].astype(F32) * mixed * _silu(gate_ref[h, sl, :].astype(F32))
            o_ref[h, sl, :] = y.astype(BF16)


def _sgu(proj3, sgu_g, w_s, b_s, *, heads, first_block, ts):
    _, m, _ = proj3.shape
    kern = functools.partial(_sgu_kernel, heads=heads, ts=ts)
    base = first_block // heads
    blk = lambda k: pl.BlockSpec((heads, ts, LANES), lambda i: (base + k, i, 0))
    b_bcast = jnp.broadcast_to(b_s[:, :, None], (heads, CHUNK, LANES))
    return pl.pallas_call(
        kern,
        out_shape=jax.ShapeDtypeStruct((heads, m, LANES), BF16),
        grid=(m // ts,),
        in_specs=[blk(0), blk(1), blk(2),
                  pl.BlockSpec((heads, 1, LANES), lambda i: (0, 0, 0)),
                  pl.BlockSpec((heads, CHUNK, CHUNK), lambda i: (0, 0, 0)),
                  pl.BlockSpec((heads, CHUNK, LANES), lambda i: (0, 0, 0))],
        out_specs=pl.BlockSpec((heads, ts, LANES), lambda i: (0, i, 0)),
        compiler_params=pltpu.CompilerParams(
            dimension_semantics=("parallel",),
            vmem_limit_bytes=VMEM_LIMIT_BYTES),
        name="spatial_gating",
    )(proj3, proj3, proj3, sgu_g.reshape(heads, 1, LANES), w_s, b_bcast)


def _pool_kernel(x_ref, halo_ref, gate_ref, w_ref, scale_ref, o_ref, *, ts, halo):
    i = pl.program_id(1)
    t = lax.broadcasted_iota(jnp.int32, (ts, ts), 0)
    j = lax.broadcasted_iota(jnp.int32, (ts, ts), 1)
    th = lax.broadcasted_iota(jnp.int32, (ts, halo), 0)
    jh = lax.broadcasted_iota(jnp.int32, (ts, halo), 1)
    pos = i * ts + lax.broadcasted_iota(jnp.int32, (ts, 1), 0)
    for g, window in enumerate(POOL_WINDOWS):
        halo_window = jnp.where(i > 0, window, 0)
        x = jnp.concatenate([x_ref[2 * g], x_ref[2 * g + 1]], axis=-1)
        xh = jnp.concatenate([halo_ref[2 * g], halo_ref[2 * g + 1]], axis=-1)
        band = jnp.where((j <= t) & (j > t - window), 1.0, 0.0).astype(BF16)
        band_h = jnp.where(jh - halo > th - halo_window, 1.0, 0.0).astype(BF16)
        wsum = (jnp.dot(band, x, preferred_element_type=F32)
                + jnp.dot(band_h, xh, preferred_element_type=F32))
        count = jnp.minimum(pos + 1, window).astype(F32)
        pooled = wsum / count - x.astype(F32)
        mapped = jnp.dot(pooled.astype(BF16), w_ref[g], preferred_element_type=F32)
        for half in range(2):
            c = 2 * g + half
            lanes = slice(half * LANES, (half + 1) * LANES)
            y = mapped[:, lanes] * scale_ref[c] * _silu(gate_ref[c].astype(F32))
            o_ref[c] = y.astype(BF16)


def _pool(proj3, w_pool_bf16, layer, pool_scale, *, batch, seq, first_block, ts):
    _, m, _ = proj3.shape
    nblk = 2 * len(POOL_WINDOWS)
    halo = LANES
    assert max(POOL_WINDOWS) <= halo and ts % halo == 0
    kern = functools.partial(_pool_kernel, ts=ts, halo=halo)
    base = first_block // nblk
    steps = seq // ts
    blk = lambda k: pl.BlockSpec((nblk, ts, LANES),
                                 lambda b, i: (base + k, b * steps + i, 0))
    halo_spec = pl.BlockSpec(
        (nblk, halo, LANES),
        lambda b, i: (base, jnp.maximum((b * seq + i * ts) // halo - 1, 0), 0))
    return pl.pallas_call(
        kern,
        out_shape=jax.ShapeDtypeStruct((nblk, m, LANES), BF16),
        grid=(batch, steps),
        in_specs=[blk(0), halo_spec, blk(1),
                  pl.BlockSpec((None,) + w_pool_bf16.shape[1:], lambda b, i: (layer, 0, 0, 0)),
                  pl.BlockSpec((nblk, 1, LANES), lambda b, i: (0, 0, 0))],
        out_specs=pl.BlockSpec((nblk, ts, LANES), lambda b, i: (0, b * steps + i, 0)),
        compiler_params=pltpu.CompilerParams(
            dimension_semantics=("parallel", "parallel"),
            vmem_limit_bytes=VMEM_LIMIT_BYTES),
        name="causal_pool",
    )(proj3, proj3, proj3, w_pool_bf16, pool_scale.reshape(nblk, 1, LANES))


def _outproj_kernel(ya_ref, yb_ref, yc_ref, w_ref, x_ref, o_ref, y_ref):
    j = pl.program_id(1)

    @pl.when(j == 0)
    def _():
        c = 0
        for ref in (ya_ref, yb_ref, yc_ref):
            for k in range(ref.shape[0]):
                y_ref[:, c * LANES:(c + 1) * LANES] = ref[k]
                c += 1

    o_ref[...] = x_ref[...] + jnp.dot(y_ref[...], w_ref[...], preferred_element_type=F32)


def _outproj(ya, yb, yc, w_bf16, layer, x2d, *, tm, tn):
    m, d = x2d.shape
    kdim = w_bf16.shape[1]
    yspec = lambda a: pl.BlockSpec((a.shape[0], tm, LANES), lambda i, j: (0, i, 0))
    return pl.pallas_call(
        _outproj_kernel,
        out_shape=jax.ShapeDtypeStruct((m, d), F32),
        grid=(m // tm, d // tn),
        in_specs=[yspec(ya), yspec(yb), yspec(yc),
                  pl.BlockSpec((None, kdim, tn), lambda i, j: (layer, 0, j)),
                  pl.BlockSpec((tm, tn), lambda i, j: (i, j))],
        out_specs=pl.BlockSpec((tm, tn), lambda i, j: (i, j)),
        scratch_shapes=[pltpu.VMEM((tm, kdim), BF16)],
        compiler_params=pltpu.CompilerParams(
            dimension_semantics=("parallel", "arbitrary"),
            vmem_limit_bytes=VMEM_LIMIT_BYTES),
        name="outproj",
    )(ya, yb, yc, w_bf16, x2d)


def kernel(x, norm_g, w_in, q_norm_g, k_norm_g, sgu_norm_g, w_spatial, b_spatial,
           w_pool, pool_scale, w_out):
    batch, seq, d_model = x.shape
    depth = norm_g.shape[0]
    b_heads = w_spatial.shape[1]
    b_width = b_heads * HEAD_DIM
    c_width = pool_scale.shape[1]
    a_width = d_model - b_width - c_width
    a_heads = a_width // HEAD_DIM
    in_cols = w_in.shape[2]
    assert in_cols == 4 * a_width + 3 * b_width + 2 * c_width
    assert w_pool.shape[1:] == (len(POOL_WINDOWS), 2 * LANES, 2 * LANES)
    m = batch * seq

    w_in_bf16 = w_in.astype(BF16)
    w_out_bf16 = w_out.astype(BF16)
    w_pool_bf16 = w_pool.astype(BF16)

    x2d = x.reshape(m, d_model)
    for layer in range(depth):
        proj3 = _inproj(x2d, norm_g[layer], w_in_bf16, layer, tm=512, tn=1024)
        proj4 = proj3.reshape(in_cols // LANES, batch, seq, LANES)
        ya = _attention(proj4, q_norm_g[layer], k_norm_g[layer], heads=a_heads, tq=256)
        yb = _sgu(proj3, sgu_norm_g[layer], w_spatial[layer], b_spatial[layer],
                  heads=b_heads, first_block=4 * a_heads, ts=512)
        yc = _pool(proj3, w_pool_bf16, layer, pool_scale[layer], batch=batch, seq=seq,
                   first_block=4 * a_heads + 3 * b_heads, ts=256)
        x2d = _outproj(ya.reshape(a_heads, m, LANES), yb, yc, w_out_bf16, layer, x2d,
                       tm=512, tn=1024)
    return x2d.reshape(batch, seq, d_model)
```

```python
import functools
import math

import jax
import jax.numpy as jnp
from jax import lax
from jax.experimental import pallas as pl
from jax.experimental.pallas import tpu as pltpu

F32 = jnp.float32
BF16 = jnp.bfloat16

LANES = 128
HEAD_DIM = 128
POOL_WINDOWS = (2, 4, 8, 16)
CHUNK = 128
EPS = 1e-6
VMEM_LIMIT_BYTES = 56 * 1024 * 1024

LOG2E = 1.4426950408889634
EXP2_UNDERFLOW = -151.0


def _silu(g):
    return g / (1.0 + jnp.exp(-g))


def _rms_scale(x):
    return lax.rsqrt(jnp.mean(x * x, axis=-1, keepdims=True) + EPS)


def _inproj_kernel(x_ref, g_ref, w_ref, hg_ref, o_ref, hn_ref, *,
                   tm, tn, rows_per_step, dot_cols, head_norm_tiles):
    j = pl.program_id(1)

    @pl.when(j == 0)
    def _():
        g = g_ref[...]

        def norm_rows(r, _):
            sl = pl.ds(pl.multiple_of(r * rows_per_step, rows_per_step), rows_per_step)
            x = x_ref[sl, :]
            hn_ref[sl, :] = (x * _rms_scale(x) * g).astype(BF16)
            return 0

        lax.fori_loop(0, tm // rows_per_step, norm_rows, 0)

    def project(head_norm):
        for d in range(tn // dot_cols):
            acc = jnp.dot(hn_ref[...], w_ref[:, d * dot_cols:(d + 1) * dot_cols],
                          preferred_element_type=F32)
            for c in range(dot_cols // LANES):
                col = d * (dot_cols // LANES) + c
                a = acc[:, c * LANES:(c + 1) * LANES]
                if head_norm:
                    a = a * _rms_scale(a) * hg_ref[:, col * LANES:(col + 1) * LANES]
                o_ref[col] = a.astype(BF16)

    @pl.when(j < head_norm_tiles)
    def _():
        project(True)

    @pl.when(j >= head_norm_tiles)
    def _():
        project(False)


def _inproj(x2d, g, w_bf16, layer, head_gain, *, tm, tn):
    m, d = x2d.shape
    n = w_bf16.shape[2]
    head_norm_tiles = head_gain.shape[1] // tn
    assert head_norm_tiles * tn == head_gain.shape[1]
    kern = functools.partial(_inproj_kernel, tm=tm, tn=tn, rows_per_step=32,
                             dot_cols=2 * LANES, head_norm_tiles=head_norm_tiles)
    return pl.pallas_call(
        kern,
        out_shape=jax.ShapeDtypeStruct((n // LANES, m, LANES), BF16),
        grid=(m // tm, n // tn),
        in_specs=[
            pl.BlockSpec((tm, d), lambda i, j: (i, 0)),
            pl.BlockSpec((1, d), lambda i, j: (0, 0)),
            pl.BlockSpec((None, d, tn), lambda i, j: (layer, 0, j)),
            pl.BlockSpec((1, tn), lambda i, j: (0, jnp.minimum(j, head_norm_tiles - 1))),
        ],
        out_specs=pl.BlockSpec((tn // LANES, tm, LANES), lambda i, j: (j, i, 0)),
        scratch_shapes=[pltpu.VMEM((tm, d), BF16)],
        compiler_params=pltpu.CompilerParams(
            dimension_semantics=("parallel", "arbitrary"),
            vmem_limit_bytes=VMEM_LIMIT_BYTES),
        name="inproj",
    )(x2d, g.reshape(1, d), w_bf16, head_gain)


def _attn_kernel(q_ref, k_ref, v_ref, gate_ref, o_ref, tri_ref, *, tq, nsub):
    qi = pl.program_id(2)
    tk = tq

    @pl.when(qi == 0)
    def _():
        r = lax.broadcasted_iota(jnp.int32, (2 * tk, tk), 0)
        c = lax.broadcasted_iota(jnp.int32, (2 * tk, tk), 1)
        tri_ref[...] = jnp.where(jnp.where(r >= tk, r - tk, r) > c, 1.0, 0.0).astype(BF16)

    tri = tri_ref[...]

    def block(q, kb, carry, acc, mask, v_scale=None):
        ksl = pl.ds(pl.multiple_of(kb * tk, tk), tk)
        kblk = k_ref[ksl, :]
        vblk = v_ref[ksl, :]
        if v_scale is not None:
            vblk = vblk * v_scale
        z = lax.dot_general(q, kblk, (((1,), (1,)), ((), ())),
                            preferred_element_type=F32)
        log_beta = jnp.minimum(z, 0.0) - jnp.log(1.0 + jnp.exp2(-jnp.abs(z))) * LOG2E
        log_1m = log_beta - z
        if mask is not None:
            log_1m = jnp.where(mask, log_1m, 0.0)
        hi = log_1m.astype(BF16)
        lo = (log_1m - hi.astype(F32)).astype(BF16)
        suffix = jnp.dot(jnp.concatenate([hi, lo], axis=1), tri, preferred_element_type=F32)
        attn = jnp.exp2(log_beta + suffix + carry)
        if mask is not None:
            attn = jnp.where(mask, attn, 0.0)
        acc = acc + jnp.dot(attn.astype(BF16), vblk, preferred_element_type=F32)
        carry = carry + (suffix[:, :1] + log_1m[:, :1])
        return carry, acc

    def exists(kb):
        return (kb >= 0).astype(F32).astype(BF16)

    row = lax.broadcasted_iota(jnp.int32, (tq, tk), 0)
    col = lax.broadcasted_iota(jnp.int32, (tq, tk), 1)
    qs = [q_ref[s * tq:(s + 1) * tq, :] for s in range(nsub)]
    first = [qi * nsub + s for s in range(nsub)]
    carries, accs = [], []
    for s in range(nsub):
        carry, acc = block(qs[s], first[s], jnp.zeros((tq, 1), F32),
                           jnp.zeros((tq, HEAD_DIM), F32), col < row)
        kb = first[s] - 1
        carry, acc = block(qs[s], jnp.maximum(kb, 0), carry, acc, None,
                           v_scale=exists(kb) if s == 0 else None)
        carries.append(carry)
        accs.append(acc)

    def live(state):
        it, carries, _ = state
        alive = jnp.bool_(False)
        for s in range(nsub):
            more = jnp.logical_and(first[s] - 2 - it >= 0, jnp.max(carries[s]) > EXP2_UNDERFLOW)
            alive = jnp.logical_or(alive, more)
        return alive

    def sweep(state):
        it, carries, accs = state
        out_c, out_a = [], []
        for s in range(nsub):
            kb = first[s] - 2 - it
            carry, acc = block(qs[s], jnp.maximum(kb, 0), carries[s], accs[s], None,
                               v_scale=exists(kb))
            out_c.append(carry)
            out_a.append(acc)
        return it + 1, tuple(out_c), tuple(out_a)

    _, _, accs = lax.while_loop(live, sweep, (jnp.int32(0), tuple(carries), tuple(accs)))
    for s in range(nsub):
        rows = slice(s * tq, (s + 1) * tq)
        o_ref[rows, :] = (accs[s] * _silu(gate_ref[rows, :].astype(F32))).astype(BF16)


def _attention(proj4, *, heads, tq, nsub):
    _, b, s, _ = proj4.shape
    rows = tq * nsub
    kern = functools.partial(_attn_kernel, tq=tq, nsub=nsub)
    blk = lambda off: pl.BlockSpec((None, None, rows, LANES),
                                   lambda bi, h, qi: (off + h, bi, qi, 0))
    full = lambda off: pl.BlockSpec((None, None, s, LANES),
                                    lambda bi, h, qi: (off + h, bi, 0, 0))
    return pl.pallas_call(
        kern,
        out_shape=jax.ShapeDtypeStruct((heads, b, s, LANES), BF16),
        grid=(b, heads, s // rows),
        in_specs=[blk(0), full(heads), full(2 * heads), blk(3 * heads)],
        out_specs=pl.BlockSpec((None, None, rows, LANES),
                               lambda bi, h, qi: (h, bi, qi, 0)),
        scratch_shapes=[pltpu.VMEM((2 * tq, tq), BF16)],
        compiler_params=pltpu.CompilerParams(
            dimension_semantics=("parallel", "parallel", "arbitrary"),
            vmem_limit_bytes=VMEM_LIMIT_BYTES),
        name="stickbreak_attn",
    )(proj4, proj4, proj4, proj4)


def _sgu_kernel(u_ref, v_ref, gate_ref, g_ref, w_ref, b_ref, o_ref, *, heads, ts):
    row = lax.broadcasted_iota(jnp.int32, (CHUNK, CHUNK), 0)
    col = lax.broadcasted_iota(jnp.int32, (CHUNK, CHUNK), 1)
    for h in range(heads):
        w = jnp.where(col <= row, w_ref[h], 0.0).astype(BF16)
        bias = b_ref[h]
        g = g_ref[h]
        for c in range(ts // CHUNK):
            sl = slice(c * CHUNK, (c + 1) * CHUNK)
            v = v_ref[h, sl, :].astype(F32)
            vn = (v * _rms_scale(v) * g).astype(BF16)
            mixed = jnp.dot(w, vn, preferred_element_type=F32) + bias
            y = u_ref[h, sl, :].astype(F32) * mixed * _silu(gate_ref[h, sl, :].astype(F32))
            o_ref[h, sl, :] = y.astype(BF16)


def _sgu(proj3, sgu_g, w_s, b_s, *, heads, first_block, ts):
    _, m, _ = proj3.shape
    kern = functools.partial(_sgu_kernel, heads=heads, ts=ts)
    base = first_block // heads
    blk = lambda k: pl.BlockSpec((heads, ts, LANES), lambda i: (base + k, i, 0))
    b_bcast = jnp.broadcast_to(b_s[:, :, None], (heads, CHUNK, LANES))
    return pl.pallas_call(
        kern,
        out_shape=jax.ShapeDtypeStruct((heads, m, LANES), BF16),
        grid=(m // ts,),
        in_specs=[blk(0), blk(1), blk(2),
                  pl.BlockSpec((heads, 1, LANES), lambda i: (0, 0, 0)),
                  pl.BlockSpec((heads, CHUNK, CHUNK), lambda i: (0, 0, 0)),
                  pl.BlockSpec((heads, CHUNK, LANES), lambda i: (0, 0, 0))],
        out_specs=pl.BlockSpec((heads, ts, LANES), lambda i: (0, i, 0)),
        compiler_params=pltpu.CompilerParams(
            dimension_semantics=("parallel",),
            vmem_limit_bytes=VMEM_LIMIT_BYTES),
        name="spatial_gating",
    )(proj3, proj3, proj3, sgu_g.reshape(heads, 1, LANES), w_s, b_bcast)


def _pool_kernel(x_ref, halo_ref, gate_ref, w_ref, scale_ref, o_ref, *, ts, halo):
    i = pl.program_id(1)
    t = lax.broadcasted_iota(jnp.int32, (ts, ts), 0)
    j = lax.broadcasted_iota(jnp.int32, (ts, ts), 1)
    th = lax.broadcasted_iota(jnp.int32, (ts, halo), 0)
    jh = lax.broadcasted_iota(jnp.int32, (ts, halo), 1)
    pos = i * ts + lax.broadcasted_iota(jnp.int32, (ts, 1), 0)
    for g, window in enumerate(POOL_WINDOWS):
        halo_window = jnp.where(i > 0, window, 0)
        x = jnp.concatenate([x_ref[2 * g], x_ref[2 * g + 1]], axis=-1)
        xh = jnp.concatenate([halo_ref[2 * g], halo_ref[2 * g + 1]], axis=-1)
        band = jnp.where((j <= t) & (j > t - window), 1.0, 0.0).astype(BF16)
        band_h = jnp.where(jh - halo > th - halo_window, 1.0, 0.0).astype(BF16)
        wsum = (jnp.dot(band, x, preferred_element_type=F32)
                + jnp.dot(band_h, xh, preferred_element_type=F32))
        count = jnp.minimum(pos + 1, window).astype(F32)
        pooled = wsum / count - x.astype(F32)
        mapped = jnp.dot(pooled.astype(BF16), w_ref[g], preferred_element_type=F32)
        for half in range(2):
            c = 2 * g + half
            lanes = slice(half * LANES, (half + 1) * LANES)
            y = mapped[:, lanes] * scale_ref[c] * _silu(gate_ref[c].astype(F32))
            o_ref[c] = y.astype(BF16)


def _pool(proj3, w_pool_bf16, layer, pool_scale, *, batch, seq, first_block, ts):
    _, m, _ = proj3.shape
    nblk = 2 * len(POOL_WINDOWS)
    halo = LANES
    assert max(POOL_WINDOWS) <= halo and ts % halo == 0
    kern = functools.partial(_pool_kernel, ts=ts, halo=halo)
    base = first_block // nblk
    steps = seq // ts
    blk = lambda k: pl.BlockSpec((nblk, ts, LANES),
                                 lambda b, i: (base + k, b * steps + i, 0))
    halo_spec = pl.BlockSpec(
        (nblk, halo, LANES),
        lambda b, i: (base, jnp.maximum((b * seq + i * ts) // halo - 1, 0), 0))
    return pl.pallas_call(
        kern,
        out_shape=jax.ShapeDtypeStruct((nblk, m, LANES), BF16),
        grid=(batch, steps),
        in_specs=[blk(0), halo_spec, blk(1),
                  pl.BlockSpec((None,) + w_pool_bf16.shape[1:], lambda b, i: (layer, 0, 0, 0)),
                  pl.BlockSpec((nblk, 1, LANES), lambda b, i: (0, 0, 0))],
        out_specs=pl.BlockSpec((nblk, ts, LANES), lambda b, i: (0, b * steps + i, 0)),
        compiler_params=pltpu.CompilerParams(
            dimension_semantics=("parallel", "parallel"),
            vmem_limit_bytes=VMEM_LIMIT_BYTES),
        name="causal_pool",
    )(proj3, proj3, proj3, w_pool_bf16, pool_scale.reshape(nblk, 1, LANES))


def _outproj_kernel(ya_ref, yb_ref, yc_ref, w_ref, x_ref, o_ref, y_ref, *, tn, dot_cols):
    j = pl.program_id(1)

    @pl.when(j == 0)
    def _():
        c = 0
        for ref in (ya_ref, yb_ref, yc_ref):
            for k in range(ref.shape[0]):
                y_ref[:, c * LANES:(c + 1) * LANES] = ref[k]
                c += 1

    for d in range(tn // dot_cols):
        cols = slice(d * dot_cols, (d + 1) * dot_cols)
        o_ref[:, cols] = x_ref[:, cols] + jnp.dot(y_ref[...], w_ref[:, cols],
                                                  preferred_element_type=F32)


def _outproj(ya, yb, yc, w_bf16, layer, x2d, *, tm, tn):
    m, d = x2d.shape
    kdim = w_bf16.shape[1]
    yspec = lambda a: pl.BlockSpec((a.shape[0], tm, LANES), lambda i, j: (0, i, 0))
    return pl.pallas_call(
        functools.partial(_outproj_kernel, tn=tn, dot_cols=2 * LANES),
        out_shape=jax.ShapeDtypeStruct((m, d), F32),
        grid=(m // tm, d // tn),
        in_specs=[yspec(ya), yspec(yb), yspec(yc),
                  pl.BlockSpec((None, kdim, tn), lambda i, j: (layer, 0, j)),
                  pl.BlockSpec((tm, tn), lambda i, j: (i, j))],
        out_specs=pl.BlockSpec((tm, tn), lambda i, j: (i, j)),
        scratch_shapes=[pltpu.VMEM((tm, kdim), BF16)],
        compiler_params=pltpu.CompilerParams(
            dimension_semantics=("parallel", "arbitrary"),
            vmem_limit_bytes=VMEM_LIMIT_BYTES),
        name="outproj",
    )(ya, yb, yc, w_bf16, x2d)


def kernel(x, norm_g, w_in, q_norm_g, k_norm_g, sgu_norm_g, w_spatial, b_spatial,
           w_pool, pool_scale, w_out):
    batch, seq, d_model = x.shape
    depth = norm_g.shape[0]
    b_heads = w_spatial.shape[1]
    b_width = b_heads * HEAD_DIM
    c_width = pool_scale.shape[1]
    a_width = d_model - b_width - c_width
    a_heads = a_width // HEAD_DIM
    in_cols = w_in.shape[2]
    assert in_cols == 4 * a_width + 3 * b_width + 2 * c_width
    assert w_pool.shape[1:] == (len(POOL_WINDOWS), 2 * LANES, 2 * LANES)
    m = batch * seq

    w_in_bf16 = w_in.astype(BF16)
    w_out_bf16 = w_out.astype(BF16)
    w_pool_bf16 = w_pool.astype(BF16)

    x2d = x.reshape(m, d_model)
    for layer in range(depth):
        head_gain = jnp.concatenate([jnp.tile(q_norm_g[layer] * (LOG2E / math.sqrt(HEAD_DIM)), a_heads),
                                     jnp.tile(k_norm_g[layer], a_heads)]).reshape(1, 2 * a_width)
        proj3 = _inproj(x2d, norm_g[layer], w_in_bf16, layer, head_gain, tm=512, tn=1024)
        proj4 = proj3.reshape(in_cols // LANES, batch, seq, LANES)
        ya = _attention(proj4, heads=a_heads, tq=256, nsub=8)
        yb = _sgu(proj3, sgu_norm_g[layer], w_spatial[layer], b_spatial[layer],
                  heads=b_heads, first_block=4 * a_heads, ts=512)
        yc = _pool(proj3, w_pool_bf16, layer, pool_scale[layer], batch=batch, seq=seq,
                   first_block=4 * a_heads + 3 * b_heads, ts=256)
        x2d = _outproj(ya.reshape(a_heads, m, LANES), yb, yc, w_out_bf16, layer, x2d,
                       tm=1024, tn=512)
    return x2d.reshape(batch, seq, d_model)
```

```python
import functools
import math

import jax
import jax.numpy as jnp
from jax import lax
from jax.experimental import pallas as pl
from jax.experimental.pallas import tpu as pltpu

F32 = jnp.float32
BF16 = jnp.bfloat16

LANES = 128
HEAD_DIM = 128
POOL_WINDOWS = (2, 4, 8, 16)
CHUNK = 128
EPS = 1e-6
VMEM_LIMIT_BYTES = 56 * 1024 * 1024

LOG2E = 1.4426950408889634
EXP2_UNDERFLOW = -151.0


def _silu(g):
    return g / (1.0 + jnp.exp(-g))


def _rms_scale(x):
    return lax.rsqrt(jnp.mean(x * x, axis=-1, keepdims=True) + EPS)


def _inproj_kernel(x_ref, g_ref, w_ref, hg_ref, o_ref, hn_ref, *,
                   tm, tn, rows_per_step, dot_cols, head_norm_tiles):
    j = pl.program_id(1)

    @pl.when(j == 0)
    def _():
        g = g_ref[...]

        def norm_rows(r, _):
            sl = pl.ds(pl.multiple_of(r * rows_per_step, rows_per_step), rows_per_step)
            x = x_ref[sl, :]
            hn_ref[sl, :] = (x * _rms_scale(x) * g).astype(BF16)
            return 0

        lax.fori_loop(0, tm // rows_per_step, norm_rows, 0)

    def project(head_norm):
        for d in range(tn // dot_cols):
            acc = jnp.dot(hn_ref[...], w_ref[:, d * dot_cols:(d + 1) * dot_cols],
                          preferred_element_type=F32)
            for c in range(dot_cols // LANES):
                col = d * (dot_cols // LANES) + c
                a = acc[:, c * LANES:(c + 1) * LANES]
                if head_norm:
                    a = a * _rms_scale(a) * hg_ref[:, col * LANES:(col + 1) * LANES]
                o_ref[col] = a.astype(BF16)

    @pl.when(j < head_norm_tiles)
    def _():
        project(True)

    @pl.when(j >= head_norm_tiles)
    def _():
        project(False)


def _inproj(x2d, g, w_bf16, layer, head_gain, *, tm, tn, dot_cols):
    m, d = x2d.shape
    n = w_bf16.shape[2]
    head_norm_tiles = head_gain.shape[1] // tn
    assert head_norm_tiles * tn == head_gain.shape[1]
    kern = functools.partial(_inproj_kernel, tm=tm, tn=tn, rows_per_step=32,
                             dot_cols=dot_cols, head_norm_tiles=head_norm_tiles)
    return pl.pallas_call(
        kern,
        out_shape=jax.ShapeDtypeStruct((n // LANES, m, LANES), BF16),
        grid=(m // tm, n // tn),
        in_specs=[
            pl.BlockSpec((tm, d), lambda i, j: (i, 0)),
            pl.BlockSpec((1, d), lambda i, j: (0, 0)),
            pl.BlockSpec((None, d, tn), lambda i, j: (layer, 0, j)),
            pl.BlockSpec((1, tn), lambda i, j: (0, jnp.minimum(j, head_norm_tiles - 1))),
        ],
        out_specs=pl.BlockSpec((tn // LANES, tm, LANES), lambda i, j: (j, i, 0)),
        scratch_shapes=[pltpu.VMEM((tm, d), BF16)],
        compiler_params=pltpu.CompilerParams(
            dimension_semantics=("parallel", "arbitrary"),
            vmem_limit_bytes=VMEM_LIMIT_BYTES),
        name="inproj",
    )(x2d, g.reshape(1, d), w_bf16, head_gain)


def _attn_kernel(q_ref, k_ref, v_ref, gate_ref, o_ref, tri_ref, *, tq, nsub, split_log1m):
    qi = pl.program_id(2)
    tk = tq

    @pl.when(qi == 0)
    def _():
        r = lax.broadcasted_iota(jnp.int32, (2 * tk, tk), 0)
        c = lax.broadcasted_iota(jnp.int32, (2 * tk, tk), 1)
        tri_ref[...] = jnp.where(jnp.where(r >= tk, r - tk, r) > c, 1.0, 0.0).astype(BF16)

    tri = tri_ref[...]

    def block(q, kb, carry, acc, mask, v_scale=None):
        ksl = pl.ds(pl.multiple_of(kb * tk, tk), tk)
        kblk = k_ref[ksl, :]
        vblk = v_ref[ksl, :]
        if v_scale is not None:
            vblk = vblk * v_scale
        z = lax.dot_general(q, kblk, (((1,), (1,)), ((), ())),
                            preferred_element_type=F32)
        log_beta = jnp.minimum(z, 0.0) - jnp.log(1.0 + jnp.exp2(-jnp.abs(z))) * LOG2E
        log_1m = log_beta - z
        if mask is not None:
            log_1m = jnp.where(mask, log_1m, 0.0)
        hi = log_1m.astype(BF16)
        if split_log1m:
            lo = (log_1m - hi.astype(F32)).astype(BF16)
            suffix = jnp.dot(jnp.concatenate([hi, lo], axis=1), tri, preferred_element_type=F32)
        else:
            suffix = jnp.dot(hi, tri[:tk], preferred_element_type=F32)
        attn = jnp.exp2(log_beta + suffix + carry)
        if mask is not None:
            attn = jnp.where(mask, attn, 0.0)
        acc = acc + jnp.dot(attn.astype(BF16), vblk, preferred_element_type=F32)
        carry = carry + (suffix[:, :1] + log_1m[:, :1])
        return carry, acc

    def exists(kb):
        return (kb >= 0).astype(F32).astype(BF16)

    row = lax.broadcasted_iota(jnp.int32, (tq, tk), 0)
    col = lax.broadcasted_iota(jnp.int32, (tq, tk), 1)
    qs = [q_ref[s * tq:(s + 1) * tq, :] for s in range(nsub)]
    first = [qi * nsub + s for s in range(nsub)]
    carries, accs = [], []
    for s in range(nsub):
        carry, acc = block(qs[s], first[s], jnp.zeros((tq, 1), F32),
                           jnp.zeros((tq, HEAD_DIM), F32), col < row)
        kb = first[s] - 1
        carry, acc = block(qs[s], jnp.maximum(kb, 0), carry, acc, None,
                           v_scale=exists(kb) if s == 0 else None)
        carries.append(carry)
        accs.append(acc)

    def live(state):
        it, carries, _ = state
        alive = jnp.bool_(False)
        for s in range(nsub):
            more = jnp.logical_and(first[s] - 2 - it >= 0, jnp.max(carries[s]) > EXP2_UNDERFLOW)
            alive = jnp.logical_or(alive, more)
        return alive

    def sweep(state):
        it, carries, accs = state
        out_c, out_a = [], []
        for s in range(nsub):
            kb = first[s] - 2 - it
            carry, acc = block(qs[s], jnp.maximum(kb, 0), carries[s], accs[s], None,
                               v_scale=exists(kb))
            out_c.append(carry)
            out_a.append(acc)
        return it + 1, tuple(out_c), tuple(out_a)

    _, _, accs = lax.while_loop(live, sweep, (jnp.int32(0), tuple(carries), tuple(accs)))
    for s in range(nsub):
        rows = slice(s * tq, (s + 1) * tq)
        o_ref[rows, :] = (accs[s] * _silu(gate_ref[rows, :].astype(F32))).astype(BF16)


def _attention(proj4, *, heads, tq, nsub, split_log1m):
    _, b, s, _ = proj4.shape
    rows = tq * nsub
    kern = functools.partial(_attn_kernel, tq=tq, nsub=nsub, split_log1m=split_log1m)
    blk = lambda off: pl.BlockSpec((None, None, rows, LANES),
                                   lambda bi, h, qi: (off + h, bi, qi, 0))
    full = lambda off: pl.BlockSpec((None, None, s, LANES),
                                    lambda bi, h, qi: (off + h, bi, 0, 0))
    return pl.pallas_call(
        kern,
        out_shape=jax.ShapeDtypeStruct((heads, b, s, LANES), BF16),
        grid=(b, heads, s // rows),
        in_specs=[blk(0), full(heads), full(2 * heads), blk(3 * heads)],
        out_specs=pl.BlockSpec((None, None, rows, LANES),
                               lambda bi, h, qi: (h, bi, qi, 0)),
        scratch_shapes=[pltpu.VMEM((2 * tq, tq), BF16)],
        compiler_params=pltpu.CompilerParams(
            dimension_semantics=("parallel", "parallel", "arbitrary"),
            vmem_limit_bytes=VMEM_LIMIT_BYTES),
        name="stickbreak_attn",
    )(proj4, proj4, proj4, proj4)


def _sgu_kernel(u_ref, v_ref, gate_ref, g_ref, w_ref, b_ref, o_ref, *, heads, ts):
    row = lax.broadcasted_iota(jnp.int32, (CHUNK, CHUNK), 0)
    col = lax.broadcasted_iota(jnp.int32, (CHUNK, CHUNK), 1)
    for h in range(heads):
        w = jnp.where(col <= row, w_ref[h], 0.0).astype(BF16)
        bias = b_ref[h]
        g = g_ref[h]
        for c in range(ts // CHUNK):
            sl = slice(c * CHUNK, (c + 1) * CHUNK)
            v = v_ref[h, sl, :].astype(F32)
            vn = (v * _rms_scale(v) * g).astype(BF16)
            mixed = jnp.dot(w, vn, preferred_element_type=F32) + bias
            y = u_ref[h, sl, :].astype(F32) * mixed * _silu(gate_ref[h, sl, :].astype(F32))
            o_ref[h, sl, :] = y.astype(BF16)


def _sgu(proj3, sgu_g, w_s, b_s, *, heads, first_block, ts):
    _, m, _ = proj3.shape
    kern = functools.partial(_sgu_kernel, heads=heads, ts=ts)
    base = first_block // heads
    blk = lambda k: pl.BlockSpec((heads, ts, LANES), lambda i: (base + k, i, 0))
    b_bcast = jnp.broadcast_to(b_s[:, :, None], (heads, CHUNK, LANES))
    return pl.pallas_call(
        kern,
        out_shape=jax.ShapeDtypeStruct((heads, m, LANES), BF16),
        grid=(m // ts,),
        in_specs=[blk(0), blk(1), blk(2),
                  pl.BlockSpec((heads, 1, LANES), lambda i: (0, 0, 0)),
                  pl.BlockSpec((heads, CHUNK, CHUNK), lambda i: (0, 0, 0)),
                  pl.BlockSpec((heads, CHUNK, LANES), lambda i: (0, 0, 0))],
        out_specs=pl.BlockSpec((heads, ts, LANES), lambda i: (0, i, 0)),
        compiler_params=pltpu.CompilerParams(
            dimension_semantics=("parallel",),
            vmem_limit_bytes=VMEM_LIMIT_BYTES),
        name="spatial_gating",
    )(proj3, proj3, proj3, sgu_g.reshape(heads, 1, LANES), w_s, b_bcast)


def _pool_kernel(x_ref, halo_ref, gate_ref, w_ref, scale_ref, o_ref, *, ts, halo):
    i = pl.program_id(1)
    t = lax.broadcasted_iota(jnp.int32, (ts, ts), 0)
    j = lax.broadcasted_iota(jnp.int32, (ts, ts), 1)
    th = lax.broadcasted_iota(jnp.int32, (ts, halo), 0)
    jh = lax.broadcasted_iota(jnp.int32, (ts, halo), 1)
    pos = i * ts + lax.broadcasted_iota(jnp.int32, (ts, 1), 0)
    for g, window in enumerate(POOL_WINDOWS):
        halo_window = jnp.where(i > 0, window, 0)
        x = jnp.concatenate([x_ref[2 * g], x_ref[2 * g + 1]], axis=-1)
        xh = jnp.concatenate([halo_ref[2 * g], halo_ref[2 * g + 1]], axis=-1)
        band = jnp.where((j <= t) & (j > t - window), 1.0, 0.0).astype(BF16)
        band_h = jnp.where(jh - halo > th - halo_window, 1.0, 0.0).astype(BF16)
        wsum = (jnp.dot(band, x, preferred_element_type=F32)
                + jnp.dot(band_h, xh, preferred_element_type=F32))
        count = jnp.minimum(pos + 1, window).astype(F32)
        pooled = wsum / count - x.astype(F32)
        mapped = jnp.dot(pooled.astype(BF16), w_ref[g], preferred_element_type=F32)
        for half in range(2):
            c = 2 * g + half
            lanes = slice(half * LANES, (half + 1) * LANES)
            y = mapped[:, lanes] * scale_ref[c] * _silu(gate_ref[c].astype(F32))
            o_ref[c] = y.astype(BF16)


def _pool(proj3, w_pool_bf16, layer, pool_scale, *, batch, seq, first_block, ts):
    _, m, _ = proj3.shape
    nblk = 2 * len(POOL_WINDOWS)
    halo = LANES
    assert max(POOL_WINDOWS) <= halo and ts % halo == 0
    kern = functools.partial(_pool_kernel, ts=ts, halo=halo)
    base = first_block // nblk
    steps = seq // ts
    blk = lambda k: pl.BlockSpec((nblk, ts, LANES),
                                 lambda b, i: (base + k, b * steps + i, 0))
    halo_spec = pl.BlockSpec(
        (nblk, halo, LANES),
        lambda b, i: (base, jnp.maximum((b * seq + i * ts) // halo - 1, 0), 0))
    return pl.pallas_call(
        kern,
        out_shape=jax.ShapeDtypeStruct((nblk, m, LANES), BF16),
        grid=(batch, steps),
        in_specs=[blk(0), halo_spec, blk(1),
                  pl.BlockSpec((None,) + w_pool_bf16.shape[1:], lambda b, i: (layer, 0, 0, 0)),
                  pl.BlockSpec((nblk, 1, LANES), lambda b, i: (0, 0, 0))],
        out_specs=pl.BlockSpec((nblk, ts, LANES), lambda b, i: (0, b * steps + i, 0)),
        compiler_params=pltpu.CompilerParams(
            dimension_semantics=("parallel", "parallel"),
            vmem_limit_bytes=VMEM_LIMIT_BYTES),
        name="causal_pool",
    )(proj3, proj3, proj3, w_pool_bf16, pool_scale.reshape(nblk, 1, LANES))


def _outproj_kernel(ya_ref, yb_ref, yc_ref, w_ref, x_ref, o_ref, y_ref, *, tn, dot_cols):
    j = pl.program_id(1)

    @pl.when(j == 0)
    def _():
        c = 0
        for ref in (ya_ref, yb_ref, yc_ref):
            for k in range(ref.shape[0]):
                y_ref[:, c * LANES:(c + 1) * LANES] = ref[k]
                c += 1

    for d in range(tn // dot_cols):
        cols = slice(d * dot_cols, (d + 1) * dot_cols)
        o_ref[:, cols] = x_ref[:, cols] + jnp.dot(y_ref[...], w_ref[:, cols],
                                                  preferred_element_type=F32)


def _outproj(ya, yb, yc, w_bf16, layer, x2d, *, tm, tn):
    m, d = x2d.shape
    kdim = w_bf16.shape[1]
    yspec = lambda a: pl.BlockSpec((a.shape[0], tm, LANES), lambda i, j: (0, i, 0))
    return pl.pallas_call(
        functools.partial(_outproj_kernel, tn=tn, dot_cols=2 * LANES),
        out_shape=jax.ShapeDtypeStruct((m, d), F32),
        grid=(m // tm, d // tn),
        in_specs=[yspec(ya), yspec(yb), yspec(yc),
                  pl.BlockSpec((None, kdim, tn), lambda i, j: (layer, 0, j)),
                  pl.BlockSpec((tm, tn), lambda i, j: (i, j))],
        out_specs=pl.BlockSpec((tm, tn), lambda i, j: (i, j)),
        scratch_shapes=[pltpu.VMEM((tm, kdim), BF16)],
        compiler_params=pltpu.CompilerParams(
            dimension_semantics=("parallel", "arbitrary"),
            vmem_limit_bytes=VMEM_LIMIT_BYTES),
        name="outproj",
    )(ya, yb, yc, w_bf16, x2d)


def kernel(x, norm_g, w_in, q_norm_g, k_norm_g, sgu_norm_g, w_spatial, b_spatial,
           w_pool, pool_scale, w_out):
    batch, seq, d_model = x.shape
    depth = norm_g.shape[0]
    b_heads = w_spatial.shape[1]
    b_width = b_heads * HEAD_DIM
    c_width = pool_scale.shape[1]
    a_width = d_model - b_width - c_width
    a_heads = a_width // HEAD_DIM
    in_cols = w_in.shape[2]
    assert in_cols == 4 * a_width + 3 * b_width + 2 * c_width
    assert w_pool.shape[1:] == (len(POOL_WINDOWS), 2 * LANES, 2 * LANES)
    m = batch * seq

    w_in_bf16 = w_in.astype(BF16)
    w_out_bf16 = w_out.astype(BF16)
    w_pool_bf16 = w_pool.astype(BF16)

    x2d = x.reshape(m, d_model)
    for layer in range(depth):
        head_gain = jnp.concatenate([jnp.tile(q_norm_g[layer] * (LOG2E / math.sqrt(HEAD_DIM)), a_heads),
                                     jnp.tile(k_norm_g[layer], a_heads)]).reshape(1, 2 * a_width)
        proj3 = _inproj(x2d, norm_g[layer], w_in_bf16, layer, head_gain,
                        tm=(512, 1024)[layer], tn=(1024, 512)[layer], dot_cols=512)
        proj4 = proj3.reshape(in_cols // LANES, batch, seq, LANES)
        ya = _attention(proj4, heads=a_heads, tq=256, nsub=8, split_log1m=(layer == 0))
        yb = _sgu(proj3, sgu_norm_g[layer], w_spatial[layer], b_spatial[layer],
                  heads=b_heads, first_block=4 * a_heads, ts=(512, 1024)[layer])
        yc = _pool(proj3, w_pool_bf16, layer, pool_scale[layer], batch=batch, seq=seq,
                   first_block=4 * a_heads + 3 * b_heads, ts=(256, 512)[layer])
        x2d = _outproj(ya.reshape(a_heads, m, LANES), yb, yc, w_out_bf16, layer, x2d,
                       tm=1024, tn=512)
    return x2d.reshape(batch, seq, d_model)
```

```python
import functools
import math

import jax
import jax.numpy as jnp
from jax import lax
from jax.experimental import pallas as pl
from jax.experimental.pallas import tpu as pltpu

F32 = jnp.float32
BF16 = jnp.bfloat16

LANES = 128
HEAD_DIM = 128
POOL_WINDOWS = (2, 4, 8, 16)
CHUNK = 128
EPS = 1e-6
VMEM_LIMIT_BYTES = 56 * 1024 * 1024

LOG2E = 1.4426950408889634
EXP2_UNDERFLOW = -151.0


def _silu(g):
    return g / (1.0 + jnp.exp(-g))


def _rms_scale(x):
    return lax.rsqrt(jnp.mean(x * x, axis=-1, keepdims=True) + EPS)


def _inproj_kernel(x_ref, g_ref, w_ref, hg_ref, o_ref, hn_ref, *,
                   tm, tn, rows_per_step, dot_cols, head_norm_tiles):
    j = pl.program_id(1)

    @pl.when(j == 0)
    def _():
        g = g_ref[...]

        def norm_rows(r, _):
            sl = pl.ds(pl.multiple_of(r * rows_per_step, rows_per_step), rows_per_step)
            x = x_ref[sl, :]
            hn_ref[sl, :] = (x * _rms_scale(x) * g).astype(BF16)
            return 0

        lax.fori_loop(0, tm // rows_per_step, norm_rows, 0)

    def project(head_norm):
        for d in range(tn // dot_cols):
            acc = jnp.dot(hn_ref[...], w_ref[:, d * dot_cols:(d + 1) * dot_cols],
                          preferred_element_type=F32)
            for c in range(dot_cols // LANES):
                col = d * (dot_cols // LANES) + c
                a = acc[:, c * LANES:(c + 1) * LANES]
                if head_norm:
                    a = a * _rms_scale(a) * hg_ref[:, col * LANES:(col + 1) * LANES]
                o_ref[col] = a.astype(BF16)

    @pl.when(j < head_norm_tiles)
    def _():
        project(True)

    @pl.when(j >= head_norm_tiles)
    def _():
        project(False)


def _inproj(x2d, g, w_bf16, layer, head_gain, *, tm, tn, dot_cols):
    m, d = x2d.shape
    n = w_bf16.shape[2]
    head_norm_tiles = head_gain.shape[1] // tn
    assert head_norm_tiles * tn == head_gain.shape[1]
    kern = functools.partial(_inproj_kernel, tm=tm, tn=tn, rows_per_step=128,
                             dot_cols=dot_cols, head_norm_tiles=head_norm_tiles)
    return pl.pallas_call(
        kern,
        out_shape=jax.ShapeDtypeStruct((n // LANES, m, LANES), BF16),
        grid=(m // tm, n // tn),
        in_specs=[
            pl.BlockSpec((tm, d), lambda i, j: (i, 0)),
            pl.BlockSpec((1, d), lambda i, j: (0, 0)),
            pl.BlockSpec((None, d, tn), lambda i, j: (layer, 0, j)),
            pl.BlockSpec((1, tn), lambda i, j: (0, jnp.minimum(j, head_norm_tiles - 1))),
        ],
        out_specs=pl.BlockSpec((tn // LANES, tm, LANES), lambda i, j: (j, i, 0)),
        scratch_shapes=[pltpu.VMEM((tm, d), BF16)],
        compiler_params=pltpu.CompilerParams(
            dimension_semantics=("parallel", "arbitrary"),
            vmem_limit_bytes=VMEM_LIMIT_BYTES),
        name="inproj",
    )(x2d, g.reshape(1, d), w_bf16, head_gain)


def _cast_jobs(casts, n_steps, flat_step):
    in_specs, out_specs, out_shapes = [], [], []
    for w_f32, first_layer, n_layers in casts:
        _, rows, cols = w_f32.shape
        rb = n_layers * rows // n_steps
        assert rb * n_steps == n_layers * rows and rb % 16 == 0 and rows % rb == 0
        per_layer = rows // rb

        def block_index(*grid_indices, first=0, per_layer=per_layer):
            step = flat_step(*grid_indices)
            return (first + step // per_layer, step % per_layer, 0)

        in_specs.append(pl.BlockSpec((None, rb, cols),
                                     functools.partial(block_index, first=first_layer)))
        out_specs.append(pl.BlockSpec((None, rb, cols), block_index))
        out_shapes.append(jax.ShapeDtypeStruct((n_layers, rows, cols), BF16))
    return in_specs, out_specs, out_shapes


def _attn_kernel(q_ref, k_ref, v_ref, gate_ref, *refs, tq, nsub, n_casts):
    cast_src = refs[:n_casts]
    o_ref = refs[n_casts]
    cast_dst = refs[n_casts + 1:2 * n_casts + 1]
    tri_ref = refs[2 * n_casts + 1]
    qi = pl.program_id(2)
    tk = tq
    first = qi * nsub

    @pl.when(qi == 0)
    def _():
        r = lax.broadcasted_iota(jnp.int32, (tk, tk), 0)
        c = lax.broadcasted_iota(jnp.int32, (tk, tk), 1)
        tri_ref[...] = jnp.where(r > c, 1.0, 0.0).astype(BF16)

    for src, dst in zip(cast_src, cast_dst):
        dst[...] = src[...].astype(BF16)

    tri = tri_ref[...]
    row = lax.broadcasted_iota(jnp.int32, (tq, tk), 0)
    col = lax.broadcasted_iota(jnp.int32, (tq, tk), 1)
    causal = col < row

    def key_slice(kb):
        return pl.ds(pl.multiple_of(jnp.maximum(kb, 0) * tk, tk), tk)

    def exists(kb):
        return (kb >= 0).astype(F32).astype(BF16)

    def logits(q, kb):
        return lax.dot_general(q, k_ref[key_slice(kb), :], (((1,), (1,)), ((), ())),
                               preferred_element_type=F32)

    def sticks(z, mask):
        log_beta = jnp.minimum(z, 0.0) - jnp.log(1.0 + jnp.exp2(-jnp.abs(z))) * LOG2E
        log_1m = log_beta - z
        if mask is not None:
            log_1m = jnp.where(mask, log_1m, 0.0)
        return log_beta, log_1m

    def suffix_sums(log_1m):
        return jnp.dot(log_1m.astype(BF16), tri, preferred_element_type=F32)

    def weights(log_beta, suffix, carry, mask):
        attn = jnp.exp2(log_beta + suffix if carry is None else log_beta + suffix + carry)
        if mask is not None:
            attn = jnp.where(mask, attn, 0.0)
        return attn.astype(BF16)

    def values(kb, scale):
        v = v_ref[key_slice(kb), :]
        return v if scale is None else v * scale

    n = nsub
    q_rows = [slice(max(j - 1, 0) * tq, min(j + 1, n) * tq) for j in range(n + 1)]
    z = [logits(q_ref[q_rows[j], :], first - 1 + j) for j in range(n + 1)]
    log_beta, log_1m = [], []
    for j in range(n + 1):
        parts = []
        if j >= 1:
            parts.append(sticks(z[j][:tq], causal))
        if j <= n - 1:
            parts.append(sticks(z[j][-tq:], None))
        log_beta.append([p[0] for p in parts])
        log_1m.append([p[1] for p in parts])
    suffix = []
    for j in range(n + 1):
        l1m = log_1m[j][0] if len(log_1m[j]) == 1 else jnp.concatenate(log_1m[j], axis=0)
        s = suffix_sums(l1m)
        suffix.append([s] if len(log_1m[j]) == 1 else [s[:tq], s[tq:]])
    carries = [None] * n
    accs = [None] * n
    for j in range(n, -1, -1):
        attn = []
        idx = 0
        if j >= 1:
            s = j - 1
            attn.append(weights(log_beta[j][idx], suffix[j][idx], None, causal))
            carries[s] = suffix[j][idx][:, :1] + log_1m[j][idx][:, :1]
            idx += 1
        if j <= n - 1:
            s = j
            attn.append(weights(log_beta[j][idx], suffix[j][idx], carries[s], None))
            carries[s] = carries[s] + (suffix[j][idx][:, :1] + log_1m[j][idx][:, :1])
        v = values(first - 1 + j, exists(first - 1) if j == 0 else None)
        pv = jnp.dot(attn[0] if len(attn) == 1 else jnp.concatenate(attn, axis=0), v,
                     preferred_element_type=F32)
        idx = 0
        if j >= 1:
            accs[j - 1] = pv[:tq]
            idx = 1
        if j <= n - 1:
            part = pv[idx * tq:(idx + 1) * tq]
            accs[j] = part if accs[j] is None else accs[j] + part

    def live(state):
        it, carries, _ = state
        alive = jnp.bool_(False)
        for s in range(n):
            more = jnp.logical_and(first + s - 2 - it >= 0,
                                   jnp.max(carries[s]) > EXP2_UNDERFLOW)
            alive = jnp.logical_or(alive, more)
        return alive

    def sweep(state):
        it, carries, accs = state
        out_c, out_a = [], []
        for s in range(n):
            kb = first + s - 2 - it
            lb, l1m = sticks(logits(q_ref[s * tq:(s + 1) * tq, :], kb), None)
            suf = suffix_sums(l1m)
            attn = weights(lb, suf, carries[s], None)
            out_a.append(accs[s] + jnp.dot(attn, values(kb, exists(kb)),
                                           preferred_element_type=F32))
            out_c.append(carries[s] + (suf[:, :1] + l1m[:, :1]))
        return it + 1, tuple(out_c), tuple(out_a)

    _, _, accs = lax.while_loop(live, sweep, (jnp.int32(0), tuple(carries), tuple(accs)))
    for s in range(n):
        rows = slice(s * tq, (s + 1) * tq)
        o_ref[rows, :] = (accs[s] * _silu(gate_ref[rows, :].astype(F32))).astype(BF16)


def _attention(proj4, casts, *, heads, tq, nsub):
    _, b, s, _ = proj4.shape
    rows = tq * nsub
    steps = s // rows
    cast_in, cast_out, cast_shapes = _cast_jobs(
        casts, b * heads * steps, lambda bi, h, qi: (bi * heads + h) * steps + qi)
    kern = functools.partial(_attn_kernel, tq=tq, nsub=nsub, n_casts=len(casts))
    blk = lambda off: pl.BlockSpec((None, None, rows, LANES),
                                   lambda bi, h, qi: (off + h, bi, qi, 0))
    full = lambda off: pl.BlockSpec((None, None, s, LANES),
                                    lambda bi, h, qi: (off + h, bi, 0, 0))
    outs = pl.pallas_call(
        kern,
        out_shape=[jax.ShapeDtypeStruct((heads, b, s, LANES), BF16)] + cast_shapes,
        grid=(b, heads, steps),
        in_specs=[blk(0), full(heads), full(2 * heads), blk(3 * heads)] + cast_in,
        out_specs=[pl.BlockSpec((None, None, rows, LANES),
                                lambda bi, h, qi: (h, bi, qi, 0))] + cast_out,
        scratch_shapes=[pltpu.VMEM((tq, tq), BF16)],
        compiler_params=pltpu.CompilerParams(
            dimension_semantics=("parallel", "parallel", "arbitrary"),
            vmem_limit_bytes=VMEM_LIMIT_BYTES),
        name="stickbreak_attn",
    )(proj4, proj4, proj4, proj4, *[c[0] for c in casts])
    return outs[0], outs[1:]


def _sgu_kernel(u_ref, v_ref, gate_ref, g_ref, w_ref, b_ref, o_ref, *, heads, ts):
    row = lax.broadcasted_iota(jnp.int32, (CHUNK, CHUNK), 0)
    col = lax.broadcasted_iota(jnp.int32, (CHUNK, CHUNK), 1)
    for h in range(heads):
        w = jnp.where(col <= row, w_ref[h], 0.0).astype(BF16)
        bias = b_ref[h]
        g = g_ref[h]
        for c in range(ts // CHUNK):
            sl = slice(c * CHUNK, (c + 1) * CHUNK)
            v = v_ref[h, sl, :].astype(F32)
            vn = (v * _rms_scale(v) * g).astype(BF16)
            mixed = jnp.dot(w, vn, preferred_element_type=F32) + bias
            y = u_ref[h, sl, :].astype(F32) * mixed * _silu(gate_ref[h, sl, :].astype(F32))
            o_ref[h, sl, :] = y.astype(BF16)


def _sgu(proj3, sgu_g, w_s, b_s, *, heads, first_block, ts):
    _, m, _ = proj3.shape
    kern = functools.partial(_sgu_kernel, heads=heads, ts=ts)
    base = first_block // heads
    blk = lambda k: pl.BlockSpec((heads, ts, LANES), lambda i: (base + k, i, 0))
    b_bcast = jnp.broadcast_to(b_s[:, :, None], (heads, CHUNK, LANES))
    return pl.pallas_call(
        kern,
        out_shape=jax.ShapeDtypeStruct((heads, m, LANES), BF16),
        grid=(m // ts,),
        in_specs=[blk(0), blk(1), blk(2),
                  pl.BlockSpec((heads, 1, LANES), lambda i: (0, 0, 0)),
                  pl.BlockSpec((heads, CHUNK, CHUNK), lambda i: (0, 0, 0)),
                  pl.BlockSpec((heads, CHUNK, LANES), lambda i: (0, 0, 0))],
        out_specs=pl.BlockSpec((heads, ts, LANES), lambda i: (0, i, 0)),
        compiler_params=pltpu.CompilerParams(
            dimension_semantics=("parallel",),
            vmem_limit_bytes=VMEM_LIMIT_BYTES),
        name="spatial_gating",
    )(proj3, proj3, proj3, sgu_g.reshape(heads, 1, LANES), w_s, b_bcast)


def _pool_kernel(x_ref, halo_ref, gate_ref, w_ref, scale_ref, o_ref, *, ts, halo):
    i = pl.program_id(1)
    t = lax.broadcasted_iota(jnp.int32, (ts, ts), 0)
    j = lax.broadcasted_iota(jnp.int32, (ts, ts), 1)
    th = lax.broadcasted_iota(jnp.int32, (ts, halo), 0)
    jh = lax.broadcasted_iota(jnp.int32, (ts, halo), 1)
    pos = i * ts + lax.broadcasted_iota(jnp.int32, (ts, 1), 0)
    for g, window in enumerate(POOL_WINDOWS):
        halo_window = jnp.where(i > 0, window, 0)
        x = jnp.concatenate([x_ref[2 * g], x_ref[2 * g + 1]], axis=-1)
        xh = jnp.concatenate([halo_ref[2 * g], halo_ref[2 * g + 1]], axis=-1)
        band = jnp.where((j <= t) & (j > t - window), 1.0, 0.0).astype(BF16)
        band_h = jnp.where(jh - halo > th - halo_window, 1.0, 0.0).astype(BF16)
        wsum = (jnp.dot(band, x, preferred_element_type=F32)
                + jnp.dot(band_h, xh, preferred_element_type=F32))
        count = jnp.minimum(pos + 1, window).astype(F32)
        pooled = wsum / count - x.astype(F32)
        mapped = jnp.dot(pooled.astype(BF16), w_ref[g], preferred_element_type=F32)
        for half in range(2):
            c = 2 * g + half
            lanes = slice(half * LANES, (half + 1) * LANES)
            y = mapped[:, lanes] * scale_ref[c] * _silu(gate_ref[c].astype(F32))
            o_ref[c] = y.astype(BF16)


def _pool(proj3, w_pool_bf16, layer, pool_scale, *, batch, seq, first_block, ts):
    _, m, _ = proj3.shape
    nblk = 2 * len(POOL_WINDOWS)
    halo = LANES
    assert max(POOL_WINDOWS) <= halo and ts % halo == 0
    kern = functools.partial(_pool_kernel, ts=ts, halo=halo)
    base = first_block // nblk
    steps = seq // ts
    blk = lambda k: pl.BlockSpec((nblk, ts, LANES),
                                 lambda b, i: (base + k, b * steps + i, 0))
    halo_spec = pl.BlockSpec(
        (nblk, halo, LANES),
        lambda b, i: (base, jnp.maximum((b * seq + i * ts) // halo - 1, 0), 0))
    return pl.pallas_call(
        kern,
        out_shape=jax.ShapeDtypeStruct((nblk, m, LANES), BF16),
        grid=(batch, steps),
        in_specs=[blk(0), halo_spec, blk(1),
                  pl.BlockSpec((None,) + w_pool_bf16.shape[1:], lambda b, i: (layer, 0, 0, 0)),
                  pl.BlockSpec((nblk, 1, LANES), lambda b, i: (0, 0, 0))],
        out_specs=pl.BlockSpec((nblk, ts, LANES), lambda b, i: (0, b * steps + i, 0)),
        compiler_params=pltpu.CompilerParams(
            dimension_semantics=("parallel", "parallel"),
            vmem_limit_bytes=VMEM_LIMIT_BYTES),
        name="causal_pool",
    )(proj3, proj3, proj3, w_pool_bf16, pool_scale.reshape(nblk, 1, LANES))


def _outproj_kernel(ya_ref, yb_ref, yc_ref, w_ref, x_ref, o_ref, y_ref, *, tn, dot_cols):
    j = pl.program_id(1)

    @pl.when(j == 0)
    def _():
        c = 0
        for ref in (ya_ref, yb_ref, yc_ref):
            for k in range(ref.shape[0]):
                y_ref[:, c * LANES:(c + 1) * LANES] = ref[k]
                c += 1

    for d in range(tn // dot_cols):
        cols = slice(d * dot_cols, (d + 1) * dot_cols)
        o_ref[:, cols] = x_ref[:, cols] + jnp.dot(y_ref[...], w_ref[:, cols],
                                                  preferred_element_type=F32)


def _outproj(ya, yb, yc, w_bf16, layer, x2d, *, tm, tn):
    m, d = x2d.shape
    kdim = w_bf16.shape[1]
    yspec = lambda a: pl.BlockSpec((a.shape[0], tm, LANES), lambda i, j: (0, i, 0))
    return pl.pallas_call(
        functools.partial(_outproj_kernel, tn=tn, dot_cols=2 * LANES),
        out_shape=jax.ShapeDtypeStruct((m, d), F32),
        grid=(m // tm, d // tn),
        in_specs=[yspec(ya), yspec(yb), yspec(yc),
                  pl.BlockSpec((None, kdim, tn), lambda i, j: (layer, 0, j)),
                  pl.BlockSpec((tm, tn), lambda i, j: (i, j))],
        out_specs=pl.BlockSpec((tm, tn), lambda i, j: (i, j)),
        scratch_shapes=[pltpu.VMEM((tm, kdim), BF16)],
        compiler_params=pltpu.CompilerParams(
            dimension_semantics=("parallel", "arbitrary"),
            vmem_limit_bytes=VMEM_LIMIT_BYTES),
        name="outproj",
    )(ya, yb, yc, w_bf16, x2d)


def kernel(x, norm_g, w_in, q_norm_g, k_norm_g, sgu_norm_g, w_spatial, b_spatial,
           w_pool, pool_scale, w_out):
    batch, seq, d_model = x.shape
    depth = norm_g.shape[0]
    b_heads = w_spatial.shape[1]
    b_width = b_heads * HEAD_DIM
    c_width = pool_scale.shape[1]
    a_width = d_model - b_width - c_width
    a_heads = a_width // HEAD_DIM
    in_cols = w_in.shape[2]
    assert in_cols == 4 * a_width + 3 * b_width + 2 * c_width
    assert w_pool.shape[1:] == (len(POOL_WINDOWS), 2 * LANES, 2 * LANES)
    m = batch * seq

    w_in_bf16 = w_in[:1].astype(BF16)
    w_pool_bf16 = w_pool.astype(BF16)
    w_out_bf16 = None

    x2d = x.reshape(m, d_model)
    for layer in range(depth):
        head_gain = jnp.concatenate([jnp.tile(q_norm_g[layer] * (LOG2E / math.sqrt(HEAD_DIM)), a_heads),
                                     jnp.tile(k_norm_g[layer], a_heads)]).reshape(1, 2 * a_width)
        proj3 = _inproj(x2d, norm_g[layer], w_in_bf16, 0, head_gain,
                        tm=512, tn=1024, dot_cols=512)
        proj4 = proj3.reshape(in_cols // LANES, batch, seq, LANES)
        casts = []
        if layer + 1 < depth:
            casts.append((w_in, layer + 1, 1))
        if layer == 0:
            casts.append((w_out, 0, depth))
        ya, casted = _attention(proj4, casts, heads=a_heads, tq=256, nsub=8)
        if layer + 1 < depth:
            w_in_bf16 = casted[0]
        if layer == 0:
            w_out_bf16 = casted[-1]
        yb = _sgu(proj3, sgu_norm_g[layer], w_spatial[layer], b_spatial[layer],
                  heads=b_heads, first_block=4 * a_heads, ts=1024)
        yc = _pool(proj3, w_pool_bf16, layer, pool_scale[layer], batch=batch, seq=seq,
                   first_block=4 * a_heads + 3 * b_heads, ts=512)
        x2d = _outproj(ya.reshape(a_heads, m, LANES), yb, yc, w_out_bf16, layer, x2d,
                       tm=1024, tn=512)
    return x2d.reshape(batch, seq, d_model)
```

```python
import functools
import math

import jax
import jax.numpy as jnp
from jax import lax
from jax.experimental import pallas as pl
from jax.experimental.pallas import tpu as pltpu

F32 = jnp.float32
BF16 = jnp.bfloat16

LANES = 128
HEAD_DIM = 128
POOL_WINDOWS = (2, 4, 8, 16)
CHUNK = 128
EPS = 1e-6
VMEM_LIMIT_BYTES = 56 * 1024 * 1024

LOG2E = 1.4426950408889634
EXP2_UNDERFLOW = -151.0


def _silu(g):
    return g / (1.0 + jnp.exp(-g))


def _rms_scale(x):
    return lax.rsqrt(jnp.mean(x * x, axis=-1, keepdims=True) + EPS)


def _inproj_kernel(x_ref, g_ref, w_ref, hg_ref, o_ref, hn_ref, r_ref, *,
                   tm, tn, rows_per_step, dot_cols, head_norm_tiles):
    j = pl.program_id(1)

    @pl.when(j == 0)
    def _():
        g = g_ref[...]

        def cast_rows(r, _):
            sl = pl.ds(pl.multiple_of(r * rows_per_step, rows_per_step), rows_per_step)
            hn_ref[sl, :] = (x_ref[sl, :] * g).astype(BF16)
            return 0

        lax.fori_loop(0, tm // rows_per_step, cast_rows, 0)

    def project(head_norm, first):
        if first:
            for r in range(tm // rows_per_step):
                rows = slice(r * rows_per_step, (r + 1) * rows_per_step)
                r_ref[rows, :] = jnp.broadcast_to(_rms_scale(x_ref[rows, :]),
                                                  (rows_per_step, LANES))
        for d in range(tn // dot_cols):
            acc = jnp.dot(hn_ref[...], w_ref[:, d * dot_cols:(d + 1) * dot_cols],
                          preferred_element_type=F32)
            for c in range(dot_cols // LANES):
                col = d * (dot_cols // LANES) + c
                a = acc[:, c * LANES:(c + 1) * LANES] * r_ref[...]
                if head_norm:
                    a = a * _rms_scale(a) * hg_ref[:, col * LANES:(col + 1) * LANES]
                o_ref[col] = a.astype(BF16)

    @pl.when(j == 0)
    def _():
        project(True, True)

    @pl.when(jnp.logical_and(j > 0, j < head_norm_tiles))
    def _():
        project(True, False)

    @pl.when(j >= head_norm_tiles)
    def _():
        project(False, False)


def _inproj(x2d, g, w_bf16, layer, head_gain, *, tm, tn, dot_cols):
    m, d = x2d.shape
    n = w_bf16.shape[2]
    head_norm_tiles = head_gain.shape[1] // tn
    assert head_norm_tiles * tn == head_gain.shape[1]
    kern = functools.partial(_inproj_kernel, tm=tm, tn=tn, rows_per_step=128,
                             dot_cols=dot_cols, head_norm_tiles=head_norm_tiles)
    return pl.pallas_call(
        kern,
        out_shape=jax.ShapeDtypeStruct((n // LANES, m, LANES), BF16),
        grid=(m // tm, n // tn),
        in_specs=[
            pl.BlockSpec((tm, d), lambda i, j: (i, 0)),
            pl.BlockSpec((1, d), lambda i, j: (0, 0)),
            pl.BlockSpec((None, d, tn), lambda i, j: (layer, 0, j)),
            pl.BlockSpec((1, tn), lambda i, j: (0, jnp.minimum(j, head_norm_tiles - 1))),
        ],
        out_specs=pl.BlockSpec((tn // LANES, tm, LANES), lambda i, j: (j, i, 0)),
        scratch_shapes=[pltpu.VMEM((tm, d), BF16), pltpu.VMEM((tm, LANES), F32)],
        compiler_params=pltpu.CompilerParams(
            dimension_semantics=("parallel", "arbitrary"),
            vmem_limit_bytes=VMEM_LIMIT_BYTES),
        name="inproj",
    )(x2d, g.reshape(1, d), w_bf16, head_gain)


def _cast_jobs(casts, n_steps, flat_step):
    in_specs, out_specs, out_shapes = [], [], []
    for w_f32, first_layer, n_layers in casts:
        _, rows, cols = w_f32.shape
        rb = n_layers * rows // n_steps
        assert rb * n_steps == n_layers * rows and rb % 16 == 0 and rows % rb == 0
        per_layer = rows // rb

        def block_index(*grid_indices, first=0, per_layer=per_layer):
            step = flat_step(*grid_indices)
            return (first + step // per_layer, step % per_layer, 0)

        in_specs.append(pl.BlockSpec((None, rb, cols),
                                     functools.partial(block_index, first=first_layer)))
        out_specs.append(pl.BlockSpec((None, rb, cols), block_index))
        out_shapes.append(jax.ShapeDtypeStruct((n_layers, rows, cols), BF16))
    return in_specs, out_specs, out_shapes


def _attn_kernel(q_ref, k_ref, v_ref, gate_ref, *refs, tq, nsub, n_casts):
    cast_src = refs[:n_casts]
    o_ref = refs[n_casts]
    cast_dst = refs[n_casts + 1:2 * n_casts + 1]
    tri_ref = refs[2 * n_casts + 1]
    qi = pl.program_id(2)
    tk = tq
    first = qi * nsub

    @pl.when(qi == 0)
    def _():
        r = lax.broadcasted_iota(jnp.int32, (tk, tk), 0)
        c = lax.broadcasted_iota(jnp.int32, (tk, tk), 1)
        tri_ref[...] = jnp.where(r > c, 1.0, 0.0).astype(BF16)

    for src, dst in zip(cast_src, cast_dst):
        dst[...] = src[...].astype(BF16)

    tri = tri_ref[...]
    row = lax.broadcasted_iota(jnp.int32, (tq, tk), 0)
    col = lax.broadcasted_iota(jnp.int32, (tq, tk), 1)
    causal = col < row

    def key_slice(kb):
        return pl.ds(pl.multiple_of(jnp.maximum(kb, 0) * tk, tk), tk)

    def exists(kb):
        return (kb >= 0).astype(F32).astype(BF16)

    def logits(q, kb):
        return lax.dot_general(q, k_ref[key_slice(kb), :], (((1,), (1,)), ((), ())),
                               preferred_element_type=F32)

    def sticks(z, mask):
        log_beta = jnp.minimum(z, 0.0) - jnp.log(1.0 + jnp.exp2(-jnp.abs(z))) * LOG2E
        log_1m = log_beta - z
        if mask is not None:
            log_1m = jnp.where(mask, log_1m, 0.0)
        return log_beta, log_1m

    def suffix_sums(log_1m):
        return jnp.dot(log_1m.astype(BF16), tri, preferred_element_type=F32)

    def weights(log_beta, suffix, carry, mask):
        attn = jnp.exp2(log_beta + suffix if carry is None else log_beta + suffix + carry)
        if mask is not None:
            attn = jnp.where(mask, attn, 0.0)
        return attn.astype(BF16)

    def values(kb, scale):
        v = v_ref[key_slice(kb), :]
        return v if scale is None else v * scale

    n = nsub
    q_rows = [slice(max(j - 1, 0) * tq, min(j + 1, n) * tq) for j in range(n + 1)]
    z, log_beta, log_1m, suffix = {}, {}, {}, {}
    carries = [None] * n
    accs = [None] * n

    def stage_logits(j):
        z[j] = logits(q_ref[q_rows[j], :], first - 1 + j)

    def stage_sticks(j):
        parts = []
        if j >= 1:
            parts.append(sticks(z[j][:tq], causal))
        if j <= n - 1:
            parts.append(sticks(z[j][-tq:], None))
        log_beta[j] = [p[0] for p in parts]
        log_1m[j] = [p[1] for p in parts]

    def stage_suffix(j):
        both = len(log_1m[j]) == 2
        s = suffix_sums(jnp.concatenate(log_1m[j], axis=0) if both else log_1m[j][0])
        suffix[j] = [s[:tq], s[tq:]] if both else [s]

    def stage_values(j):
        attn = []
        idx = 0
        if j >= 1:
            s = j - 1
            attn.append(weights(log_beta[j][idx], suffix[j][idx], None, causal))
            carries[s] = suffix[j][idx][:, :1] + log_1m[j][idx][:, :1]
            idx += 1
        if j <= n - 1:
            s = j
            attn.append(weights(log_beta[j][idx], suffix[j][idx], carries[s], None))
            carries[s] = carries[s] + (suffix[j][idx][:, :1] + log_1m[j][idx][:, :1])
        v = values(first - 1 + j, exists(first - 1) if j == 0 else None)
        pv = jnp.dot(attn[0] if len(attn) == 1 else jnp.concatenate(attn, axis=0), v,
                     preferred_element_type=F32)
        idx = 0
        if j >= 1:
            accs[j - 1] = pv[:tq]
            idx = 1
        if j <= n - 1:
            part = pv[idx * tq:(idx + 1) * tq]
            accs[j] = part if accs[j] is None else accs[j] + part

    stages = ((stage_logits, 0), (stage_sticks, 1), (stage_suffix, 1), (stage_values, 2))
    for t in range(n + 1 + stages[-1][1]):
        for stage, lag in stages:
            j = n - (t - lag)
            if 0 <= j <= n:
                stage(j)

    def live(state):
        it, carries, _ = state
        alive = jnp.bool_(False)
        for s in range(n):
            more = jnp.logical_and(first + s - 2 - it >= 0,
                                   jnp.max(carries[s]) > EXP2_UNDERFLOW)
            alive = jnp.logical_or(alive, more)
        return alive

    def sweep(state):
        it, carries, accs = state
        out_c, out_a = [], []
        for s in range(n):
            kb = first + s - 2 - it
            lb, l1m = sticks(logits(q_ref[s * tq:(s + 1) * tq, :], kb), None)
            suf = suffix_sums(l1m)
            attn = weights(lb, suf, carries[s], None)
            out_a.append(accs[s] + jnp.dot(attn, values(kb, exists(kb)),
                                           preferred_element_type=F32))
            out_c.append(carries[s] + (suf[:, :1] + l1m[:, :1]))
        return it + 1, tuple(out_c), tuple(out_a)

    _, _, accs = lax.while_loop(live, sweep, (jnp.int32(0), tuple(carries), tuple(accs)))
    for s in range(n):
        rows = slice(s * tq, (s + 1) * tq)
        o_ref[rows, :] = (accs[s] * _silu(gate_ref[rows, :].astype(F32))).astype(BF16)


def _attention(proj4, casts, *, heads, tq, nsub):
    _, b, s, _ = proj4.shape
    rows = tq * nsub
    steps = s // rows
    cast_in, cast_out, cast_shapes = _cast_jobs(
        casts, b * heads * steps, lambda bi, h, qi: (bi * heads + h) * steps + qi)
    kern = functools.partial(_attn_kernel, tq=tq, nsub=nsub, n_casts=len(casts))
    blk = lambda off: pl.BlockSpec((None, None, rows, LANES),
                                   lambda bi, h, qi: (off + h, bi, qi, 0))
    full = lambda off: pl.BlockSpec((None, None, s, LANES),
                                    lambda bi, h, qi: (off + h, bi, 0, 0))
    outs = pl.pallas_call(
        kern,
        out_shape=[jax.ShapeDtypeStruct((heads, b, s, LANES), BF16)] + cast_shapes,
        grid=(b, heads, steps),
        in_specs=[blk(0), full(heads), full(2 * heads), blk(3 * heads)] + cast_in,
        out_specs=[pl.BlockSpec((None, None, rows, LANES),
                                lambda bi, h, qi: (h, bi, qi, 0))] + cast_out,
        scratch_shapes=[pltpu.VMEM((tq, tq), BF16)],
        compiler_params=pltpu.CompilerParams(
            dimension_semantics=("parallel", "parallel", "arbitrary"),
            vmem_limit_bytes=VMEM_LIMIT_BYTES),
        name="stickbreak_attn",
    )(proj4, proj4, proj4, proj4, *[c[0] for c in casts])
    return outs[0], outs[1:]


def _sgu_kernel(u_ref, v_ref, gate_ref, g_ref, w_ref, b_ref, o_ref, *, heads, ts):
    row = lax.broadcasted_iota(jnp.int32, (CHUNK, CHUNK), 0)
    col = lax.broadcasted_iota(jnp.int32, (CHUNK, CHUNK), 1)
    for h in range(heads):
        w = jnp.where(col <= row, w_ref[h], 0.0).astype(BF16)
        bias = b_ref[h]
        g = g_ref[h]
        for c in range(ts // CHUNK):
            sl = slice(c * CHUNK, (c + 1) * CHUNK)
            v = v_ref[h, sl, :].astype(F32)
            vn = (v * _rms_scale(v) * g).astype(BF16)
            mixed = jnp.dot(w, vn, preferred_element_type=F32) + bias
            y = u_ref[h, sl, :].astype(F32) * mixed * _silu(gate_ref[h, sl, :].astype(F32))
            o_ref[h, sl, :] = y.astype(BF16)


def _sgu(proj3, sgu_g, w_s, b_s, *, heads, first_block, ts):
    _, m, _ = proj3.shape
    kern = functools.partial(_sgu_kernel, heads=heads, ts=ts)
    base = first_block // heads
    blk = lambda k: pl.BlockSpec((heads, ts, LANES), lambda i: (base + k, i, 0))
    b_bcast = jnp.broadcast_to(b_s[:, :, None], (heads, CHUNK, LANES))
    return pl.pallas_call(
        kern,
        out_shape=jax.ShapeDtypeStruct((heads, m, LANES), BF16),
        grid=(m // ts,),
        in_specs=[blk(0), blk(1), blk(2),
                  pl.BlockSpec((heads, 1, LANES), lambda i: (0, 0, 0)),
                  pl.BlockSpec((heads, CHUNK, CHUNK), lambda i: (0, 0, 0)),
                  pl.BlockSpec((heads, CHUNK, LANES), lambda i: (0, 0, 0))],
        out_specs=pl.BlockSpec((heads, ts, LANES), lambda i: (0, i, 0)),
        compiler_params=pltpu.CompilerParams(
            dimension_semantics=("parallel",),
            vmem_limit_bytes=VMEM_LIMIT_BYTES),
        name="spatial_gating",
    )(proj3, proj3, proj3, sgu_g.reshape(heads, 1, LANES), w_s, b_bcast)


def _pool_kernel(x_ref, halo_ref, gate_ref, w_ref, scale_ref, o_ref, band_ref, band_h_ref, *,
                 ts, halo):
    i = pl.program_id(1)

    @pl.when(jnp.logical_and(pl.program_id(0) == 0, i == 0))
    def _():
        t = lax.broadcasted_iota(jnp.int32, (ts, ts), 0)
        j = lax.broadcasted_iota(jnp.int32, (ts, ts), 1)
        th = lax.broadcasted_iota(jnp.int32, (ts, halo), 0)
        jh = lax.broadcasted_iota(jnp.int32, (ts, halo), 1)
        for g, window in enumerate(POOL_WINDOWS):
            band_ref[g] = jnp.where((j <= t) & (j > t - window), 1.0, 0.0).astype(BF16)
            band_h_ref[g] = jnp.where(jh - halo > th - window, 1.0, 0.0).astype(BF16)

    pos = i * ts + lax.broadcasted_iota(jnp.int32, (ts, 1), 0)
    has_prev = (i > 0).astype(F32).astype(BF16)
    groups = range(len(POOL_WINDOWS))
    x = [jnp.concatenate([x_ref[2 * g], x_ref[2 * g + 1]], axis=-1) for g in groups]
    xh = [jnp.concatenate([halo_ref[2 * g], halo_ref[2 * g + 1]], axis=-1) * has_prev
          for g in groups]
    wsum = [jnp.dot(band_ref[g], x[g], preferred_element_type=F32)
            + jnp.dot(band_h_ref[g], xh[g], preferred_element_type=F32) for g in groups]
    pooled = [wsum[g] / jnp.minimum(pos + 1, window).astype(F32) - x[g].astype(F32)
              for g, window in enumerate(POOL_WINDOWS)]
    mapped = [jnp.dot(pooled[g].astype(BF16), w_ref[g], preferred_element_type=F32)
              for g in groups]
    for g in groups:
        for half in range(2):
            c = 2 * g + half
            lanes = slice(half * LANES, (half + 1) * LANES)
            y = mapped[g][:, lanes] * scale_ref[c] * _silu(gate_ref[c].astype(F32))
            o_ref[c] = y.astype(BF16)


def _pool(proj3, w_pool_bf16, layer, pool_scale, *, batch, seq, first_block, ts):
    _, m, _ = proj3.shape
    nblk = 2 * len(POOL_WINDOWS)
    halo = LANES
    assert max(POOL_WINDOWS) <= halo and ts % halo == 0
    kern = functools.partial(_pool_kernel, ts=ts, halo=halo)
    base = first_block // nblk
    steps = seq // ts
    blk = lambda k: pl.BlockSpec((nblk, ts, LANES),
                                 lambda b, i: (base + k, b * steps + i, 0))
    halo_spec = pl.BlockSpec(
        (nblk, halo, LANES),
        lambda b, i: (base, jnp.maximum((b * seq + i * ts) // halo - 1, 0), 0))
    return pl.pallas_call(
        kern,
        out_shape=jax.ShapeDtypeStruct((nblk, m, LANES), BF16),
        grid=(batch, steps),
        in_specs=[blk(0), halo_spec, blk(1),
                  pl.BlockSpec((None,) + w_pool_bf16.shape[1:], lambda b, i: (layer, 0, 0, 0)),
                  pl.BlockSpec((nblk, 1, LANES), lambda b, i: (0, 0, 0))],
        out_specs=pl.BlockSpec((nblk, ts, LANES), lambda b, i: (0, b * steps + i, 0)),
        scratch_shapes=[pltpu.VMEM((len(POOL_WINDOWS), ts, ts), BF16),
                        pltpu.VMEM((len(POOL_WINDOWS), ts, halo), BF16)],
        compiler_params=pltpu.CompilerParams(
            dimension_semantics=("arbitrary", "arbitrary"),
            vmem_limit_bytes=VMEM_LIMIT_BYTES),
        name="causal_pool",
    )(proj3, proj3, proj3, w_pool_bf16, pool_scale.reshape(nblk, 1, LANES))


def _outproj_kernel(ya_ref, yb_ref, yc_ref, w_ref, x_ref, o_ref, y_ref, *, tn, dot_cols):
    j = pl.program_id(1)

    @pl.when(j == 0)
    def _():
        c = 0
        for ref in (ya_ref, yb_ref, yc_ref):
            for k in range(ref.shape[0]):
                y_ref[:, c * LANES:(c + 1) * LANES] = ref[k]
                c += 1

    for d in range(tn // dot_cols):
        cols = slice(d * dot_cols, (d + 1) * dot_cols)
        o_ref[:, cols] = x_ref[:, cols] + jnp.dot(y_ref[...], w_ref[:, cols],
                                                  preferred_element_type=F32)


def _outproj(ya, yb, yc, w_bf16, layer, x2d, *, tm, tn):
    m, d = x2d.shape
    kdim = w_bf16.shape[1]
    yspec = lambda a: pl.BlockSpec((a.shape[0], tm, LANES), lambda i, j: (0, i, 0))
    return pl.pallas_call(
        functools.partial(_outproj_kernel, tn=tn, dot_cols=2 * LANES),
        out_shape=jax.ShapeDtypeStruct((m, d), F32),
        grid=(m // tm, d // tn),
        in_specs=[yspec(ya), yspec(yb), yspec(yc),
                  pl.BlockSpec((None, kdim, tn), lambda i, j: (layer, 0, j)),
                  pl.BlockSpec((tm, tn), lambda i, j: (i, j))],
        out_specs=pl.BlockSpec((tm, tn), lambda i, j: (i, j)),
        scratch_shapes=[pltpu.VMEM((tm, kdim), BF16)],
        compiler_params=pltpu.CompilerParams(
            dimension_semantics=("parallel", "arbitrary"),
            vmem_limit_bytes=VMEM_LIMIT_BYTES),
        name="outproj",
    )(ya, yb, yc, w_bf16, x2d)


def kernel(x, norm_g, w_in, q_norm_g, k_norm_g, sgu_norm_g, w_spatial, b_spatial,
           w_pool, pool_scale, w_out):
    batch, seq, d_model = x.shape
    depth = norm_g.shape[0]
    b_heads = w_spatial.shape[1]
    b_width = b_heads * HEAD_DIM
    c_width = pool_scale.shape[1]
    a_width = d_model - b_width - c_width
    a_heads = a_width // HEAD_DIM
    in_cols = w_in.shape[2]
    assert in_cols == 4 * a_width + 3 * b_width + 2 * c_width
    assert w_pool.shape[1:] == (len(POOL_WINDOWS), 2 * LANES, 2 * LANES)
    m = batch * seq

    w_in_bf16 = w_in[:1].astype(BF16)
    w_pool_bf16 = w_pool.astype(BF16)
    w_out_bf16 = None

    x2d = x.reshape(m, d_model)
    for layer in range(depth):
        head_gain = jnp.concatenate([jnp.tile(q_norm_g[layer] * (LOG2E / math.sqrt(HEAD_DIM)), a_heads),
                                     jnp.tile(k_norm_g[layer], a_heads)]).reshape(1, 2 * a_width)
        proj3 = _inproj(x2d, norm_g[layer], w_in_bf16, 0, head_gain,
                        tm=512, tn=1024, dot_cols=512)
        proj4 = proj3.reshape(in_cols // LANES, batch, seq, LANES)
        casts = []
        if layer + 1 < depth:
            casts.append((w_in, layer + 1, 1))
        if layer == 0:
            casts.append((w_out, 0, depth))
        ya, casted = _attention(proj4, casts, heads=a_heads, tq=256, nsub=8)
        if layer + 1 < depth:
            w_in_bf16 = casted[0]
        if layer == 0:
            w_out_bf16 = casted[-1]
        yb = _sgu(proj3, sgu_norm_g[layer], w_spatial[layer], b_spatial[layer],
                  heads=b_heads, first_block=4 * a_heads, ts=1024)
        yc = _pool(proj3, w_pool_bf16, layer, pool_scale[layer], batch=batch, seq=seq,
                   first_block=4 * a_heads + 3 * b_heads, ts=512)
        x2d = _outproj(ya.reshape(a_heads, m, LANES), yb, yc, w_out_bf16, layer, x2d,
                       tm=1024, tn=512)
    return x2d.reshape(batch, seq, d_model)
```

```python
import functools
import math

import jax
import jax.numpy as jnp
from jax import lax
from jax.experimental import pallas as pl
from jax.experimental.pallas import tpu as pltpu

F32 = jnp.float32
BF16 = jnp.bfloat16

LANES = 128
HEAD_DIM = 128
POOL_WINDOWS = (2, 4, 8, 16)
CHUNK = 128
EPS = 1e-6
VMEM_LIMIT_BYTES = 56 * 1024 * 1024

LOG2E = 1.4426950408889634
EXP2_UNDERFLOW = -151.0


def _silu(g):
    return g / (1.0 + jnp.exp(-g))


def _rms_scale(x):
    return lax.rsqrt(jnp.mean(x * x, axis=-1, keepdims=True) + EPS)


def _inproj_kernel(x_ref, g_ref, w_ref, hg_ref, o_ref, hn_ref, r_ref, *,
                   tm, tn, rows_per_step, dot_cols, head_norm_tiles):
    j = pl.program_id(1)

    @pl.when(j == 0)
    def _():
        g = g_ref[...]

        def cast_rows(r, _):
            sl = pl.ds(pl.multiple_of(r * rows_per_step, rows_per_step), rows_per_step)
            hn_ref[sl, :] = (x_ref[sl, :] * g).astype(BF16)
            return 0

        lax.fori_loop(0, tm // rows_per_step, cast_rows, 0)

    def project(head_norm, first):
        if first:
            for r in range(tm // rows_per_step):
                rows = slice(r * rows_per_step, (r + 1) * rows_per_step)
                r_ref[rows, :] = jnp.broadcast_to(_rms_scale(x_ref[rows, :]),
                                                  (rows_per_step, LANES))
        for d in range(tn // dot_cols):
            acc = jnp.dot(hn_ref[...], w_ref[:, d * dot_cols:(d + 1) * dot_cols],
                          preferred_element_type=F32)
            for c in range(dot_cols // LANES):
                col = d * (dot_cols // LANES) + c
                a = acc[:, c * LANES:(c + 1) * LANES] * r_ref[...]
                if head_norm:
                    a = a * _rms_scale(a) * hg_ref[:, col * LANES:(col + 1) * LANES]
                o_ref[col] = a.astype(BF16)

    @pl.when(j == 0)
    def _():
        project(True, True)

    @pl.when(jnp.logical_and(j > 0, j < head_norm_tiles))
    def _():
        project(True, False)

    @pl.when(j >= head_norm_tiles)
    def _():
        project(False, False)


def _inproj(x2d, g, w_bf16, layer, head_gain, *, tm, tn, dot_cols):
    m, d = x2d.shape
    n = w_bf16.shape[2]
    head_norm_tiles = head_gain.shape[1] // tn
    assert head_norm_tiles * tn == head_gain.shape[1]
    kern = functools.partial(_inproj_kernel, tm=tm, tn=tn, rows_per_step=128,
                             dot_cols=dot_cols, head_norm_tiles=head_norm_tiles)
    return pl.pallas_call(
        kern,
        out_shape=jax.ShapeDtypeStruct((n // LANES, m, LANES), BF16),
        grid=(m // tm, n // tn),
        in_specs=[
            pl.BlockSpec((tm, d), lambda i, j: (i, 0)),
            pl.BlockSpec((1, d), lambda i, j: (0, 0)),
            pl.BlockSpec((None, d, tn), lambda i, j: (layer, 0, j)),
            pl.BlockSpec((1, tn), lambda i, j: (0, jnp.minimum(j, head_norm_tiles - 1))),
        ],
        out_specs=pl.BlockSpec((tn // LANES, tm, LANES), lambda i, j: (j, i, 0)),
        scratch_shapes=[pltpu.VMEM((tm, d), BF16), pltpu.VMEM((tm, LANES), F32)],
        compiler_params=pltpu.CompilerParams(
            dimension_semantics=("parallel", "arbitrary"),
            vmem_limit_bytes=VMEM_LIMIT_BYTES),
        name="inproj",
    )(x2d, g.reshape(1, d), w_bf16, head_gain)


def _cast_jobs(casts, n_steps, flat_step):
    in_specs, out_specs, out_shapes = [], [], []
    for w_f32, first_layer, n_layers in casts:
        _, rows, cols = w_f32.shape
        rb = n_layers * rows // n_steps
        assert rb * n_steps == n_layers * rows and rb % 16 == 0 and rows % rb == 0
        per_layer = rows // rb

        def block_index(*grid_indices, first=0, per_layer=per_layer):
            step = flat_step(*grid_indices)
            return (first + step // per_layer, step % per_layer, 0)

        in_specs.append(pl.BlockSpec((None, rb, cols),
                                     functools.partial(block_index, first=first_layer)))
        out_specs.append(pl.BlockSpec((None, rb, cols), block_index))
        out_shapes.append(jax.ShapeDtypeStruct((n_layers, rows, cols), BF16))
    return in_specs, out_specs, out_shapes


def _attn_kernel(q_ref, k_ref, v_ref, gate_ref, *refs, tq, nsub, n_casts):
    cast_src = refs[:n_casts]
    o_ref = refs[n_casts]
    cast_dst = refs[n_casts + 1:2 * n_casts + 1]
    tri_ref = refs[2 * n_casts + 1]
    qi = pl.program_id(2)
    tk = tq
    first = qi * nsub

    @pl.when(qi == 0)
    def _():
        r = lax.broadcasted_iota(jnp.int32, (tk, tk), 0)
        c = lax.broadcasted_iota(jnp.int32, (tk, tk), 1)
        tri_ref[...] = jnp.where(r > c, 1.0, 0.0).astype(BF16)

    for src, dst in zip(cast_src, cast_dst):
        dst[...] = src[...].astype(BF16)

    tri = tri_ref[...]
    row = lax.broadcasted_iota(jnp.int32, (tq, tk), 0)
    col = lax.broadcasted_iota(jnp.int32, (tq, tk), 1)
    causal = col < row

    def key_slice(kb):
        return pl.ds(pl.multiple_of(jnp.maximum(kb, 0) * tk, tk), tk)

    def exists(kb):
        return (kb >= 0).astype(F32).astype(BF16)

    def logits(q, kb):
        return lax.dot_general(q, k_ref[key_slice(kb), :], (((1,), (1,)), ((), ())),
                               preferred_element_type=F32)

    def sticks(z, mask):
        log_beta = jnp.minimum(z, 0.0) - jnp.log(1.0 + jnp.exp2(-jnp.abs(z))) * LOG2E
        log_1m = log_beta - z
        if mask is not None:
            log_1m = jnp.where(mask, log_1m, 0.0)
        return log_beta, log_1m

    def suffix_sums(log_1m):
        return jnp.dot(log_1m.astype(BF16), tri, preferred_element_type=F32)

    def weights(log_beta, suffix, carry, mask):
        attn = jnp.exp2(log_beta + suffix if carry is None else log_beta + suffix + carry)
        if mask is not None:
            attn = jnp.where(mask, attn, 0.0)
        return attn.astype(BF16)

    def values(kb, scale):
        v = v_ref[key_slice(kb), :]
        return v if scale is None else v * scale

    n = nsub
    q_rows = [slice(max(j - 1, 0) * tq, min(j + 1, n) * tq) for j in range(n + 1)]
    z, log_beta, log_1m, suffix = {}, {}, {}, {}
    carries = [None] * n
    accs = [None] * n

    def stage_logits(j):
        z[j] = logits(q_ref[q_rows[j], :], first - 1 + j)

    def stage_sticks(j):
        parts = []
        if j >= 1:
            parts.append(sticks(z[j][:tq], causal))
        if j <= n - 1:
            parts.append(sticks(z[j][-tq:], None))
        log_beta[j] = [p[0] for p in parts]
        log_1m[j] = [p[1] for p in parts]

    def stage_suffix(j):
        both = len(log_1m[j]) == 2
        s = suffix_sums(jnp.concatenate(log_1m[j], axis=0) if both else log_1m[j][0])
        suffix[j] = [s[:tq], s[tq:]] if both else [s]

    def stage_values(j):
        attn = []
        idx = 0
        if j >= 1:
            s = j - 1
            attn.append(weights(log_beta[j][idx], suffix[j][idx], None, causal))
            carries[s] = suffix[j][idx][:, :1] + log_1m[j][idx][:, :1]
            idx += 1
        if j <= n - 1:
            s = j
            attn.append(weights(log_beta[j][idx], suffix[j][idx], carries[s], None))
            carries[s] = carries[s] + (suffix[j][idx][:, :1] + log_1m[j][idx][:, :1])
        v = values(first - 1 + j, exists(first - 1) if j == 0 else None)
        pv = jnp.dot(attn[0] if len(attn) == 1 else jnp.concatenate(attn, axis=0), v,
                     preferred_element_type=F32)
        idx = 0
        if j >= 1:
            accs[j - 1] = pv[:tq]
            idx = 1
        if j <= n - 1:
            part = pv[idx * tq:(idx + 1) * tq]
            accs[j] = part if accs[j] is None else accs[j] + part

    stages = ((stage_logits, 0), (stage_sticks, 1), (stage_suffix, 1), (stage_values, 2))
    for t in range(n + 1 + stages[-1][1]):
        for stage, lag in stages:
            j = n - (t - lag)
            if 0 <= j <= n:
                stage(j)

    def live(state):
        it, carries, _ = state
        alive = jnp.bool_(False)
        for s in range(n):
            more = jnp.logical_and(first + s - 2 - it >= 0,
                                   jnp.max(carries[s]) > EXP2_UNDERFLOW)
            alive = jnp.logical_or(alive, more)
        return alive

    def sweep(state):
        it, carries, accs = state
        out_c, out_a = [], []
        for s in range(n):
            kb = first + s - 2 - it
            lb, l1m = sticks(logits(q_ref[s * tq:(s + 1) * tq, :], kb), None)
            suf = suffix_sums(l1m)
            attn = weights(lb, suf, carries[s], None)
            out_a.append(accs[s] + jnp.dot(attn, values(kb, exists(kb)),
                                           preferred_element_type=F32))
            out_c.append(carries[s] + (suf[:, :1] + l1m[:, :1]))
        return it + 1, tuple(out_c), tuple(out_a)

    _, _, accs = lax.while_loop(live, sweep, (jnp.int32(0), tuple(carries), tuple(accs)))
    for s in range(n):
        rows = slice(s * tq, (s + 1) * tq)
        o_ref[rows, :] = (accs[s] * _silu(gate_ref[rows, :].astype(F32))).astype(BF16)


def _attention(proj4, casts, *, heads, tq, nsub):
    _, b, s, _ = proj4.shape
    rows = tq * nsub
    steps = s // rows
    cast_in, cast_out, cast_shapes = _cast_jobs(
        casts, b * heads * steps, lambda bi, h, qi: (bi * heads + h) * steps + qi)
    kern = functools.partial(_attn_kernel, tq=tq, nsub=nsub, n_casts=len(casts))
    blk = lambda off: pl.BlockSpec((None, None, rows, LANES),
                                   lambda bi, h, qi: (off + h, bi, qi, 0))
    full = lambda off: pl.BlockSpec((None, None, s, LANES),
                                    lambda bi, h, qi: (off + h, bi, 0, 0))
    outs = pl.pallas_call(
        kern,
        out_shape=[jax.ShapeDtypeStruct((heads, b, s, LANES), BF16)] + cast_shapes,
        grid=(b, heads, steps),
        in_specs=[blk(0), full(heads), full(2 * heads), blk(3 * heads)] + cast_in,
        out_specs=[pl.BlockSpec((None, None, rows, LANES),
                                lambda bi, h, qi: (h, bi, qi, 0))] + cast_out,
        scratch_shapes=[pltpu.VMEM((tq, tq), BF16)],
        compiler_params=pltpu.CompilerParams(
            dimension_semantics=("parallel", "parallel", "arbitrary"),
            vmem_limit_bytes=VMEM_LIMIT_BYTES),
        name="stickbreak_attn",
    )(proj4, proj4, proj4, proj4, *[c[0] for c in casts])
    return outs[0], outs[1:]


def _sgu_head(h, u_ref, v_ref, gate_ref, g_ref, w_ref, b_ref, write, *, ts):
    row = lax.broadcasted_iota(jnp.int32, (CHUNK, CHUNK), 0)
    col = lax.broadcasted_iota(jnp.int32, (CHUNK, CHUNK), 1)
    w = jnp.where(col <= row, w_ref[h], 0.0).astype(BF16)
    bias = b_ref[h]
    g = g_ref[h]
    chunks = [slice(c * CHUNK, (c + 1) * CHUNK) for c in range(ts // CHUNK)]
    v = v_ref[h].astype(F32)
    vn = (v * _rms_scale(v) * g).astype(BF16)
    mixed = jnp.dot(w, jnp.concatenate([vn[sl] for sl in chunks], axis=1),
                    preferred_element_type=F32)
    tiles = []
    for c, sl in enumerate(chunks):
        m_c = mixed[:, c * LANES:(c + 1) * LANES] + bias
        y = u_ref[h, sl, :].astype(F32) * m_c * _silu(gate_ref[h, sl, :].astype(F32))
        tiles.append(y.astype(BF16))
    write(h, jnp.concatenate(tiles, axis=0))


def _pool_bands(band_ref, band_h_ref, *, ts, halo):
    t = lax.broadcasted_iota(jnp.int32, (ts, ts), 0)
    j = lax.broadcasted_iota(jnp.int32, (ts, ts), 1)
    th = lax.broadcasted_iota(jnp.int32, (ts, halo), 0)
    jh = lax.broadcasted_iota(jnp.int32, (ts, halo), 1)
    for g, window in enumerate(POOL_WINDOWS):
        band_ref[g] = jnp.where((j <= t) & (j > t - window), 1.0, 0.0).astype(BF16)
        band_h_ref[g] = jnp.where(jh - halo > th - window, 1.0, 0.0).astype(BF16)


def _pool_group(g, x_ref, halo_ref, gate_ref, w_ref, scale_ref, band_ref, band_h_ref, write, *,
                ts, tile_in_seq):
    window = POOL_WINDOWS[g]
    pos = tile_in_seq * ts + lax.broadcasted_iota(jnp.int32, (ts, 1), 0)
    has_prev = (tile_in_seq > 0).astype(F32).astype(BF16)
    x = jnp.concatenate([x_ref[2 * g], x_ref[2 * g + 1]], axis=-1)
    xh = jnp.concatenate([halo_ref[2 * g], halo_ref[2 * g + 1]], axis=-1) * has_prev
    wsum = (jnp.dot(band_ref[g], x, preferred_element_type=F32)
            + jnp.dot(band_h_ref[g], xh, preferred_element_type=F32))
    pooled = wsum / jnp.minimum(pos + 1, window).astype(F32) - x.astype(F32)
    mapped = jnp.dot(pooled.astype(BF16), w_ref[g], preferred_element_type=F32)
    for half in range(2):
        c = 2 * g + half
        lanes = slice(half * LANES, (half + 1) * LANES)
        y = mapped[:, lanes] * scale_ref[c] * _silu(gate_ref[c].astype(F32))
        write(c, y.astype(BF16))


def _mix_outproj_kernel(ya_ref, u_ref, v_ref, gb_ref, sg_ref, ws_ref, bs_ref,
                        xc_ref, halo_ref, gc_ref, wp_ref, ps_ref, w_ref, x_ref,
                        o_ref, y_ref, band_ref, band_h_ref, *,
                        tm, tn, dot_cols, halo, tiles_per_seq):
    i = pl.program_id(0)
    j = pl.program_id(1)
    heads_a, heads_b = ya_ref.shape[0], u_ref.shape[0]
    a_cols = heads_a * LANES
    b_cols = heads_b * LANES

    @pl.when(jnp.logical_and(i == 0, j == 0))
    def _():
        _pool_bands(band_ref, band_h_ref, ts=tm, halo=halo)

    def y_writer(first_col):
        def write(block, tile):
            lo = first_col + block * LANES
            y_ref[:, lo:lo + LANES] = tile
        return write

    @pl.when(j == 0)
    def _():
        for k in range(heads_a):
            y_ref[:, k * LANES:(k + 1) * LANES] = ya_ref[k]
        tile_in_seq = lax.rem(i, tiles_per_seq)
        mixer_jobs = [functools.partial(_sgu_head, h, u_ref, v_ref, gb_ref, sg_ref, ws_ref,
                                        bs_ref, y_writer(a_cols), ts=tm)
                      for h in range(heads_b)]
        mixer_jobs += [functools.partial(_pool_group, g, xc_ref, halo_ref, gc_ref, wp_ref, ps_ref,
                                         band_ref, band_h_ref, y_writer(a_cols + b_cols),
                                         ts=tm, tile_in_seq=tile_in_seq)
                       for g in range(len(POOL_WINDOWS))]
        n_dots = tn // dot_cols
        per_dot = -(-len(mixer_jobs) // n_dots)
        head = []
        for d in range(n_dots):
            head.append(jnp.dot(y_ref[:, :a_cols],
                                w_ref[:a_cols, d * dot_cols:(d + 1) * dot_cols],
                                preferred_element_type=F32))
            for job in mixer_jobs[d * per_dot:(d + 1) * per_dot]:
                job()
        for d in range(tn // dot_cols):
            cols = slice(d * dot_cols, (d + 1) * dot_cols)
            o_ref[:, cols] = x_ref[:, cols] + head[d] + jnp.dot(
                y_ref[:, a_cols:], w_ref[a_cols:, cols], preferred_element_type=F32)

    @pl.when(j > 0)
    def _():
        for d in range(tn // dot_cols):
            cols = slice(d * dot_cols, (d + 1) * dot_cols)
            o_ref[:, cols] = x_ref[:, cols] + jnp.dot(y_ref[...], w_ref[:, cols],
                                                      preferred_element_type=F32)


def _mix_outproj(ya, proj3, sgu_g, w_s, b_s, w_pool_bf16, pool_scale, w_bf16, layer, x2d, *,
                 seq, b_first_block, c_first_block, tm, tn):
    m, d = x2d.shape
    kdim = w_bf16.shape[1]
    heads_a = ya.shape[0]
    heads_b = w_s.shape[0]
    nblk_c = 2 * len(POOL_WINDOWS)
    halo = LANES
    assert max(POOL_WINDOWS) <= halo and tm % halo == 0 and seq % tm == 0
    assert (heads_a + heads_b + nblk_c) * LANES == kdim
    b_base = b_first_block // heads_b
    c_base = c_first_block // nblk_c
    rows = lambda nb, base: pl.BlockSpec((nb, tm, LANES), lambda i, j: (base, i, 0))
    whole = lambda shape: pl.BlockSpec(shape, lambda i, j: (0,) * len(shape))
    b_bcast = jnp.broadcast_to(b_s[:, :, None], (heads_b, CHUNK, LANES))
    kern = functools.partial(_mix_outproj_kernel, tm=tm, tn=tn, dot_cols=2 * LANES, halo=halo,
                             tiles_per_seq=seq // tm)
    return pl.pallas_call(
        kern,
        out_shape=jax.ShapeDtypeStruct((m, d), F32),
        grid=(m // tm, d // tn),
        in_specs=[rows(heads_a, 0),
                  rows(heads_b, b_base), rows(heads_b, b_base + 1), rows(heads_b, b_base + 2),
                  whole((heads_b, 1, LANES)), whole((heads_b, CHUNK, CHUNK)),
                  whole((heads_b, CHUNK, LANES)),
                  rows(nblk_c, c_base),
                  pl.BlockSpec((nblk_c, halo, LANES),
                               lambda i, j: (c_base, jnp.maximum(i * (tm // halo) - 1, 0), 0)),
                  rows(nblk_c, c_base + 1),
                  pl.BlockSpec((None,) + w_pool_bf16.shape[1:], lambda i, j: (layer, 0, 0, 0)),
                  whole((nblk_c, 1, LANES)),
                  pl.BlockSpec((None, kdim, tn), lambda i, j: (layer, 0, j)),
                  pl.BlockSpec((tm, tn), lambda i, j: (i, j))],
        out_specs=pl.BlockSpec((tm, tn), lambda i, j: (i, j)),
        scratch_shapes=[pltpu.VMEM((tm, kdim), BF16),
                        pltpu.VMEM((len(POOL_WINDOWS), tm, tm), BF16),
                        pltpu.VMEM((len(POOL_WINDOWS), tm, halo), BF16)],
        compiler_params=pltpu.CompilerParams(
            dimension_semantics=("arbitrary", "arbitrary"),
            vmem_limit_bytes=VMEM_LIMIT_BYTES),
        name="mix_outproj",
    )(ya, proj3, proj3, proj3, sgu_g.reshape(heads_b, 1, LANES), w_s, b_bcast,
      proj3, proj3, proj3, w_pool_bf16, pool_scale.reshape(nblk_c, 1, LANES), w_bf16, x2d)


def kernel(x, norm_g, w_in, q_norm_g, k_norm_g, sgu_norm_g, w_spatial, b_spatial,
           w_pool, pool_scale, w_out):
    batch, seq, d_model = x.shape
    depth = norm_g.shape[0]
    b_heads = w_spatial.shape[1]
    b_width = b_heads * HEAD_DIM
    c_width = pool_scale.shape[1]
    a_width = d_model - b_width - c_width
    a_heads = a_width // HEAD_DIM
    in_cols = w_in.shape[2]
    assert in_cols == 4 * a_width + 3 * b_width + 2 * c_width
    assert w_pool.shape[1:] == (len(POOL_WINDOWS), 2 * LANES, 2 * LANES)
    m = batch * seq

    w_in_bf16 = w_in[:1].astype(BF16)
    w_pool_bf16 = w_pool.astype(BF16)
    w_out_bf16 = None

    x2d = x.reshape(m, d_model)
    for layer in range(depth):
        head_gain = jnp.concatenate([jnp.tile(q_norm_g[layer] * (LOG2E / math.sqrt(HEAD_DIM)), a_heads),
                                     jnp.tile(k_norm_g[layer], a_heads)]).reshape(1, 2 * a_width)
        proj3 = _inproj(x2d, norm_g[layer], w_in_bf16, 0, head_gain,
                        tm=512, tn=1024, dot_cols=512)
        proj4 = proj3.reshape(in_cols // LANES, batch, seq, LANES)
        casts = []
        if layer + 1 < depth:
            casts.append((w_in, layer + 1, 1))
        if layer == 0:
            casts.append((w_out, 0, depth))
        ya, casted = _attention(proj4, casts, heads=a_heads, tq=256, nsub=16)
        if layer + 1 < depth:
            w_in_bf16 = casted[0]
        if layer == 0:
            w_out_bf16 = casted[-1]
        x2d = _mix_outproj(ya.reshape(a_heads, m, LANES), proj3, sgu_norm_g[layer],
                           w_spatial[layer], b_spatial[layer], w_pool_bf16, pool_scale[layer],
                           w_out_bf16, layer, x2d, seq=seq, b_first_block=4 * a_heads,
                           c_first_block=4 * a_heads + 3 * b_heads, tm=512, tn=1024)
    return x2d.reshape(batch, seq, d_model)
```

```python
import functools
import math

import jax
import jax.numpy as jnp
from jax import lax
from jax.experimental import pallas as pl
from jax.experimental.pallas import tpu as pltpu

F32 = jnp.float32
BF16 = jnp.bfloat16

LANES = 128
HEAD_DIM = 128
POOL_WINDOWS = (2, 4, 8, 16)
CHUNK = 128
EPS = 1e-6
VMEM_LIMIT_BYTES = 56 * 1024 * 1024

LOG2E = 1.4426950408889634
EXP2_UNDERFLOW = -151.0


def _silu(g):
    return g / (1.0 + jnp.exp(-g))


def _rms_scale(x):
    return lax.rsqrt(jnp.mean(x * x, axis=-1, keepdims=True) + EPS)


def _inproj_kernel(x_ref, g_ref, w_ref, hg_ref, o_ref, hn_ref, r_ref, *,
                   tm, tn, rows_per_step, dot_cols, head_norm_tiles):
    j = pl.program_id(1)

    @pl.when(j == 0)
    def _():
        g = g_ref[...]

        def cast_rows(r, _):
            sl = pl.ds(pl.multiple_of(r * rows_per_step, rows_per_step), rows_per_step)
            hn_ref[sl, :] = (x_ref[sl, :] * g).astype(BF16)
            return 0

        lax.fori_loop(0, tm // rows_per_step, cast_rows, 0)

    def project(head_norm, first):
        if first:
            for r in range(tm // rows_per_step):
                rows = slice(r * rows_per_step, (r + 1) * rows_per_step)
                r_ref[rows, :] = jnp.broadcast_to(_rms_scale(x_ref[rows, :]),
                                                  (rows_per_step, LANES))
        for d in range(tn // dot_cols):
            acc = jnp.dot(hn_ref[...], w_ref[:, d * dot_cols:(d + 1) * dot_cols],
                          preferred_element_type=F32)
            for c in range(dot_cols // LANES):
                col = d * (dot_cols // LANES) + c
                a = acc[:, c * LANES:(c + 1) * LANES] * r_ref[...]
                if head_norm:
                    a = a * _rms_scale(a) * hg_ref[:, col * LANES:(col + 1) * LANES]
                o_ref[col] = a.astype(BF16)

    @pl.when(j == 0)
    def _():
        project(True, True)

    @pl.when(jnp.logical_and(j > 0, j < head_norm_tiles))
    def _():
        project(True, False)

    @pl.when(j >= head_norm_tiles)
    def _():
        project(False, False)


def _inproj(x2d, g, w_bf16, layer, head_gain, *, tm, tn, dot_cols):
    m, d = x2d.shape
    n = w_bf16.shape[2]
    head_norm_tiles = head_gain.shape[1] // tn
    assert head_norm_tiles * tn == head_gain.shape[1]
    kern = functools.partial(_inproj_kernel, tm=tm, tn=tn, rows_per_step=128,
                             dot_cols=dot_cols, head_norm_tiles=head_norm_tiles)
    return pl.pallas_call(
        kern,
        out_shape=jax.ShapeDtypeStruct((n // LANES, m, LANES), BF16),
        grid=(m // tm, n // tn),
        in_specs=[
            pl.BlockSpec((tm, d), lambda i, j: (i, 0)),
            pl.BlockSpec((1, d), lambda i, j: (0, 0)),
            pl.BlockSpec((None, d, tn), lambda i, j: (layer, 0, j)),
            pl.BlockSpec((1, tn), lambda i, j: (0, jnp.minimum(j, head_norm_tiles - 1))),
        ],
        out_specs=pl.BlockSpec((tn // LANES, tm, LANES), lambda i, j: (j, i, 0)),
        scratch_shapes=[pltpu.VMEM((tm, d), BF16), pltpu.VMEM((tm, LANES), F32)],
        compiler_params=pltpu.CompilerParams(
            dimension_semantics=("parallel", "arbitrary"),
            vmem_limit_bytes=VMEM_LIMIT_BYTES),
        name="inproj",
    )(x2d, g.reshape(1, d), w_bf16, head_gain)


def _cast_jobs(casts, n_steps, flat_step):
    in_specs, out_specs, out_shapes = [], [], []
    for w_f32, first_layer, n_layers in casts:
        _, rows, cols = w_f32.shape
        rb = n_layers * rows // n_steps
        assert rb * n_steps == n_layers * rows and rb % 16 == 0 and rows % rb == 0
        per_layer = rows // rb

        def block_index(*grid_indices, first=0, per_layer=per_layer):
            step = flat_step(*grid_indices)
            return (first + step // per_layer, step % per_layer, 0)

        in_specs.append(pl.BlockSpec((None, rb, cols),
                                     functools.partial(block_index, first=first_layer)))
        out_specs.append(pl.BlockSpec((None, rb, cols), block_index))
        out_shapes.append(jax.ShapeDtypeStruct((n_layers, rows, cols), BF16))
    return in_specs, out_specs, out_shapes


def _attn_kernel(q_ref, k_ref, v_ref, gate_ref, *refs, tq, nsub, n_casts):
    cast_src = refs[:n_casts]
    o_ref = refs[n_casts]
    cast_dst = refs[n_casts + 1:2 * n_casts + 1]
    tri_ref = refs[2 * n_casts + 1]
    qi = pl.program_id(2)
    tk = tq
    first = qi * nsub

    @pl.when(qi == 0)
    def _():
        r = lax.broadcasted_iota(jnp.int32, (tk, tk), 0)
        c = lax.broadcasted_iota(jnp.int32, (tk, tk), 1)
        tri_ref[...] = jnp.where(r > c, 1.0, 0.0).astype(BF16)

    for src, dst in zip(cast_src, cast_dst):
        dst[...] = src[...].astype(BF16)

    tri = tri_ref[...]
    row = lax.broadcasted_iota(jnp.int32, (tq, tk), 0)
    col = lax.broadcasted_iota(jnp.int32, (tq, tk), 1)
    causal = col < row

    def key_slice(kb):
        return pl.ds(pl.multiple_of(jnp.maximum(kb, 0) * tk, tk), tk)

    def exists(kb):
        return (kb >= 0).astype(F32).astype(BF16)

    def logits(q, kb):
        return lax.dot_general(q, k_ref[key_slice(kb), :], (((1,), (1,)), ((), ())),
                               preferred_element_type=F32)

    def sticks(z, mask):
        log_beta = jnp.minimum(z, 0.0) - jnp.log(1.0 + jnp.exp2(-jnp.abs(z))) * LOG2E
        log_1m = log_beta - z
        if mask is not None:
            log_1m = jnp.where(mask, log_1m, 0.0)
        return log_beta, log_1m

    def suffix_sums(log_1m):
        return jnp.dot(log_1m.astype(BF16), tri, preferred_element_type=F32)

    def weights(log_beta, suffix, carry, mask):
        attn = jnp.exp2(log_beta + suffix if carry is None else log_beta + suffix + carry)
        if mask is not None:
            attn = jnp.where(mask, attn, 0.0)
        return attn.astype(BF16)

    def values(kb, scale):
        v = v_ref[key_slice(kb), :]
        return v if scale is None else v * scale

    n = nsub
    q_rows = [slice(max(j - 1, 0) * tq, min(j + 1, n) * tq) for j in range(n + 1)]
    z, log_beta, log_1m, suffix = {}, {}, {}, {}
    carries = [None] * n
    accs = [None] * n

    def stage_logits(j):
        z[j] = logits(q_ref[q_rows[j], :], first - 1 + j)

    def stage_sticks(j):
        parts = []
        if j >= 1:
            parts.append(sticks(z[j][:tq], causal))
        if j <= n - 1:
            parts.append(sticks(z[j][-tq:], None))
        log_beta[j] = [p[0] for p in parts]
        log_1m[j] = [p[1] for p in parts]

    def stage_suffix(j):
        both = len(log_1m[j]) == 2
        s = suffix_sums(jnp.concatenate(log_1m[j], axis=0) if both else log_1m[j][0])
        suffix[j] = [s[:tq], s[tq:]] if both else [s]

    def stage_values(j):
        attn = []
        idx = 0
        if j >= 1:
            s = j - 1
            attn.append(weights(log_beta[j][idx], suffix[j][idx], None, causal))
            carries[s] = suffix[j][idx][:, :1] + log_1m[j][idx][:, :1]
            idx += 1
        if j <= n - 1:
            s = j
            attn.append(weights(log_beta[j][idx], suffix[j][idx], carries[s], None))
            carries[s] = carries[s] + (suffix[j][idx][:, :1] + log_1m[j][idx][:, :1])
        v = values(first - 1 + j, exists(first - 1) if j == 0 else None)
        pv = jnp.dot(attn[0] if len(attn) == 1 else jnp.concatenate(attn, axis=0), v,
                     preferred_element_type=F32)
        idx = 0
        if j >= 1:
            accs[j - 1] = pv[:tq]
            idx = 1
        if j <= n - 1:
            part = pv[idx * tq:(idx + 1) * tq]
            accs[j] = part if accs[j] is None else accs[j] + part

    stages = ((stage_logits, 0), (stage_sticks, 1), (stage_suffix, 1), (stage_values, 2))
    for t in range(n + 1 + stages[-1][1]):
        for stage, lag in stages:
            j = n - (t - lag)
            if 0 <= j <= n:
                stage(j)

    def live(state):
        it, carries, _ = state
        alive = jnp.bool_(False)
        for s in range(n):
            more = jnp.logical_and(first + s - 2 - it >= 0,
                                   jnp.max(carries[s]) > EXP2_UNDERFLOW)
            alive = jnp.logical_or(alive, more)
        return alive

    def sweep(state):
        it, carries, accs = state
        out_c, out_a = [], []
        for s in range(n):
            kb = first + s - 2 - it
            lb, l1m = sticks(logits(q_ref[s * tq:(s + 1) * tq, :], kb), None)
            suf = suffix_sums(l1m)
            attn = weights(lb, suf, carries[s], None)
            out_a.append(accs[s] + jnp.dot(attn, values(kb, exists(kb)),
                                           preferred_element_type=F32))
            out_c.append(carries[s] + (suf[:, :1] + l1m[:, :1]))
        return it + 1, tuple(out_c), tuple(out_a)

    _, _, accs = lax.while_loop(live, sweep, (jnp.int32(0), tuple(carries), tuple(accs)))
    for s in range(n):
        rows = slice(s * tq, (s + 1) * tq)
        o_ref[rows, :] = (accs[s] * _silu(gate_ref[rows, :].astype(F32))).astype(BF16)


def _attention(proj4, casts, *, heads, tq, nsub):
    _, b, s, _ = proj4.shape
    rows = tq * nsub
    steps = s // rows
    cast_in, cast_out, cast_shapes = _cast_jobs(
        casts, b * heads * steps, lambda bi, h, qi: (bi * heads + h) * steps + qi)
    kern = functools.partial(_attn_kernel, tq=tq, nsub=nsub, n_casts=len(casts))
    blk = lambda off: pl.BlockSpec((None, None, rows, LANES),
                                   lambda bi, h, qi: (off + h, bi, qi, 0))
    full = lambda off: pl.BlockSpec((None, None, s, LANES),
                                    lambda bi, h, qi: (off + h, bi, 0, 0))
    outs = pl.pallas_call(
        kern,
        out_shape=[jax.ShapeDtypeStruct((heads, b, s, LANES), BF16)] + cast_shapes,
        grid=(b, heads, steps),
        in_specs=[blk(0), full(heads), full(2 * heads), blk(3 * heads)] + cast_in,
        out_specs=[pl.BlockSpec((None, None, rows, LANES),
                                lambda bi, h, qi: (h, bi, qi, 0))] + cast_out,
        scratch_shapes=[pltpu.VMEM((tq, tq), BF16)],
        compiler_params=pltpu.CompilerParams(
            dimension_semantics=("parallel", "parallel", "arbitrary"),
            vmem_limit_bytes=VMEM_LIMIT_BYTES),
        name="stickbreak_attn",
    )(proj4, proj4, proj4, proj4, *[c[0] for c in casts])
    return outs[0], outs[1:]


def _sgu_kernel(u_ref, v_ref, gate_ref, g_ref, w_ref, b_ref, o_ref, *, heads, ts):
    row = lax.broadcasted_iota(jnp.int32, (CHUNK, CHUNK), 0)
    col = lax.broadcasted_iota(jnp.int32, (CHUNK, CHUNK), 1)
    for h in range(heads):
        w = jnp.where(col <= row, w_ref[h], 0.0).astype(BF16)
        bias = b_ref[h]
        g = g_ref[h]
        for c in range(ts // CHUNK):
            sl = slice(c * CHUNK, (c + 1) * CHUNK)
            v = v_ref[h, sl, :].astype(F32)
            vn = (v * _rms_scale(v) * g).astype(BF16)
            mixed = jnp.dot(w, vn, preferred_element_type=F32) + bias
            y = u_ref[h, sl, :].astype(F32) * mixed * _silu(gate_ref[h, sl, :].astype(F32))
            o_ref[h, sl, :] = y.astype(BF16)


def _sgu(proj3, sgu_g, w_s, b_s, *, heads, first_block, ts):
    _, m, _ = proj3.shape
    kern = functools.partial(_sgu_kernel, heads=heads, ts=ts)
    base = first_block // heads
    blk = lambda k: pl.BlockSpec((heads, ts, LANES), lambda i: (base + k, i, 0))
    b_bcast = jnp.broadcast_to(b_s[:, :, None], (heads, CHUNK, LANES))
    return pl.pallas_call(
        kern,
        out_shape=jax.ShapeDtypeStruct((heads, m, LANES), BF16),
        grid=(m // ts,),
        in_specs=[blk(0), blk(1), blk(2),
                  pl.BlockSpec((heads, 1, LANES), lambda i: (0, 0, 0)),
                  pl.BlockSpec((heads, CHUNK, CHUNK), lambda i: (0, 0, 0)),
                  pl.BlockSpec((heads, CHUNK, LANES), lambda i: (0, 0, 0))],
        out_specs=pl.BlockSpec((heads, ts, LANES), lambda i: (0, i, 0)),
        compiler_params=pltpu.CompilerParams(
            dimension_semantics=("parallel",),
            vmem_limit_bytes=VMEM_LIMIT_BYTES),
        name="spatial_gating",
    )(proj3, proj3, proj3, sgu_g.reshape(heads, 1, LANES), w_s, b_bcast)


def _pool_kernel(x_ref, halo_ref, gate_ref, w_ref, scale_ref, o_ref, band_ref, band_h_ref, *,
                 ts, halo):
    i = pl.program_id(1)

    @pl.when(jnp.logical_and(pl.program_id(0) == 0, i == 0))
    def _():
        t = lax.broadcasted_iota(jnp.int32, (ts, ts), 0)
        j = lax.broadcasted_iota(jnp.int32, (ts, ts), 1)
        th = lax.broadcasted_iota(jnp.int32, (ts, halo), 0)
        jh = lax.broadcasted_iota(jnp.int32, (ts, halo), 1)
        for g, window in enumerate(POOL_WINDOWS):
            band_ref[g] = jnp.where((j <= t) & (j > t - window), 1.0, 0.0).astype(BF16)
            band_h_ref[g] = jnp.where(jh - halo > th - window, 1.0, 0.0).astype(BF16)

    pos = i * ts + lax.broadcasted_iota(jnp.int32, (ts, 1), 0)
    has_prev = (i > 0).astype(F32).astype(BF16)
    groups = range(len(POOL_WINDOWS))
    x = [jnp.concatenate([x_ref[2 * g], x_ref[2 * g + 1]], axis=-1) for g in groups]
    xh = [jnp.concatenate([halo_ref[2 * g], halo_ref[2 * g + 1]], axis=-1) * has_prev
          for g in groups]
    wsum = [jnp.dot(band_ref[g], x[g], preferred_element_type=F32)
            + jnp.dot(band_h_ref[g], xh[g], preferred_element_type=F32) for g in groups]
    pooled = [wsum[g] / jnp.minimum(pos + 1, window).astype(F32) - x[g].astype(F32)
              for g, window in enumerate(POOL_WINDOWS)]
    mapped = [jnp.dot(pooled[g].astype(BF16), w_ref[g], preferred_element_type=F32)
              for g in groups]
    for g in groups:
        for half in range(2):
            c = 2 * g + half
            lanes = slice(half * LANES, (half + 1) * LANES)
            y = mapped[g][:, lanes] * scale_ref[c] * _silu(gate_ref[c].astype(F32))
            o_ref[c] = y.astype(BF16)


def _pool(proj3, w_pool_bf16, layer, pool_scale, *, batch, seq, first_block, ts):
    _, m, _ = proj3.shape
    nblk = 2 * len(POOL_WINDOWS)
    halo = LANES
    assert max(POOL_WINDOWS) <= halo and ts % halo == 0
    kern = functools.partial(_pool_kernel, ts=ts, halo=halo)
    base = first_block // nblk
    steps = seq // ts
    blk = lambda k: pl.BlockSpec((nblk, ts, LANES),
                                 lambda b, i: (base + k, b * steps + i, 0))
    halo_spec = pl.BlockSpec(
        (nblk, halo, LANES),
        lambda b, i: (base, jnp.maximum((b * seq + i * ts) // halo - 1, 0), 0))
    return pl.pallas_call(
        kern,
        out_shape=jax.ShapeDtypeStruct((nblk, m, LANES), BF16),
        grid=(batch, steps),
        in_specs=[blk(0), halo_spec, blk(1),
                  pl.BlockSpec((None,) + w_pool_bf16.shape[1:], lambda b, i: (layer, 0, 0, 0)),
                  pl.BlockSpec((nblk, 1, LANES), lambda b, i: (0, 0, 0))],
        out_specs=pl.BlockSpec((nblk, ts, LANES), lambda b, i: (0, b * steps + i, 0)),
        scratch_shapes=[pltpu.VMEM((len(POOL_WINDOWS), ts, ts), BF16),
                        pltpu.VMEM((len(POOL_WINDOWS), ts, halo), BF16)],
        compiler_params=pltpu.CompilerParams(
            dimension_semantics=("arbitrary", "arbitrary"),
            vmem_limit_bytes=VMEM_LIMIT_BYTES),
        name="causal_pool",
    )(proj3, proj3, proj3, w_pool_bf16, pool_scale.reshape(nblk, 1, LANES))


def _outproj_kernel(ya_ref, yb_ref, yc_ref, w_ref, x_ref, o_ref, y_ref, *, tn, dot_cols):
    j = pl.program_id(1)

    @pl.when(j == 0)
    def _():
        c = 0
        for ref in (ya_ref, yb_ref, yc_ref):
            for k in range(ref.shape[0]):
                y_ref[:, c * LANES:(c + 1) * LANES] = ref[k]
                c += 1

    for d in range(tn // dot_cols):
        cols = slice(d * dot_cols, (d + 1) * dot_cols)
        o_ref[:, cols] = x_ref[:, cols] + jnp.dot(y_ref[...], w_ref[:, cols],
                                                  preferred_element_type=F32)


def _outproj(ya, yb, yc, w_bf16, layer, x2d, *, tm, tn):
    m, d = x2d.shape
    kdim = w_bf16.shape[1]
    yspec = lambda a: pl.BlockSpec((a.shape[0], tm, LANES), lambda i, j: (0, i, 0))
    return pl.pallas_call(
        functools.partial(_outproj_kernel, tn=tn, dot_cols=2 * LANES),
        out_shape=jax.ShapeDtypeStruct((m, d), F32),
        grid=(m // tm, d // tn),
        in_specs=[yspec(ya), yspec(yb), yspec(yc),
                  pl.BlockSpec((None, kdim, tn), lambda i, j: (layer, 0, j)),
                  pl.BlockSpec((tm, tn), lambda i, j: (i, j))],
        out_specs=pl.BlockSpec((tm, tn), lambda i, j: (i, j)),
        scratch_shapes=[pltpu.VMEM((tm, kdim), BF16)],
        compiler_params=pltpu.CompilerParams(
            dimension_semantics=("parallel", "arbitrary"),
            vmem_limit_bytes=VMEM_LIMIT_BYTES),
        name="outproj",
    )(ya, yb, yc, w_bf16, x2d)


def kernel(x, norm_g, w_in, q_norm_g, k_norm_g, sgu_norm_g, w_spatial, b_spatial,
           w_pool, pool_scale, w_out):
    batch, seq, d_model = x.shape
    depth = norm_g.shape[0]
    b_heads = w_spatial.shape[1]
    b_width = b_heads * HEAD_DIM
    c_width = pool_scale.shape[1]
    a_width = d_model - b_width - c_width
    a_heads = a_width // HEAD_DIM
    in_cols = w_in.shape[2]
    assert in_cols == 4 * a_width + 3 * b_width + 2 * c_width
    assert w_pool.shape[1:] == (len(POOL_WINDOWS), 2 * LANES, 2 * LANES)
    m = batch * seq

    w_in_bf16 = w_in[:1].astype(BF16)
    w_pool_bf16 = w_pool.astype(BF16)
    w_out_bf16 = None

    x2d = x.reshape(m, d_model)
    for layer in range(depth):
        head_gain = jnp.concatenate([jnp.tile(q_norm_g[layer] * (LOG2E / math.sqrt(HEAD_DIM)), a_heads),
                                     jnp.tile(k_norm_g[layer], a_heads)]).reshape(1, 2 * a_width)
        proj3 = _inproj(x2d, norm_g[layer], w_in_bf16, 0, head_gain,
                        tm=512, tn=1024, dot_cols=512)
        proj4 = proj3.reshape(in_cols // LANES, batch, seq, LANES)
        casts = []
        if layer + 1 < depth:
            casts.append((w_in, layer + 1, 1))
        if layer == 0:
            casts.append((w_out, 0, depth))
        ya, casted = _attention(proj4, casts, heads=a_heads, tq=256, nsub=16)
        if layer + 1 < depth:
            w_in_bf16 = casted[0]
        if layer == 0:
            w_out_bf16 = casted[-1]
        yb = _sgu(proj3, sgu_norm_g[layer], w_spatial[layer], b_spatial[layer],
                  heads=b_heads, first_block=4 * a_heads, ts=1024)
        yc = _pool(proj3, w_pool_bf16, layer, pool_scale[layer], batch=batch, seq=seq,
                   first_block=4 * a_heads + 3 * b_heads, ts=512)
        x2d = _outproj(ya.reshape(a_heads, m, LANES), yb, yc, w_out_bf16, layer, x2d,
                       tm=1024, tn=512)
    return x2d.reshape(batch, seq, d_model)
```

```python
import functools
import math

import jax
import jax.numpy as jnp
from jax import lax
from jax.experimental import pallas as pl
from jax.experimental.pallas import tpu as pltpu

F32 = jnp.float32
BF16 = jnp.bfloat16

LANES = 128
HEAD_DIM = 128
POOL_WINDOWS = (2, 4, 8, 16)
CHUNK = 128
EPS = 1e-6
VMEM_LIMIT_BYTES = 56 * 1024 * 1024

TILES = {
    "inproj": dict(tm=512, tn=1024, dot_cols=512),
    "attention": dict(tq=256, nsub=16),
    "sgu": dict(ts=1024),
    "pool": dict(ts=512),
    "outproj": dict(tm=1024, tn=512),
}

LOG2E = 1.4426950408889634
EXP2_UNDERFLOW = -151.0


def _silu(g):
    return g / (1.0 + jnp.exp(-g))


def _rms_scale(x):
    return lax.rsqrt(jnp.mean(x * x, axis=-1, keepdims=True) + EPS)


def _inproj_kernel(x_ref, g_ref, w_ref, hg_ref, o_ref, hn_ref, r_ref, *,
                   tm, tn, rows_per_step, dot_cols, head_norm_tiles):
    j = pl.program_id(1)

    @pl.when(j == 0)
    def _():
        g = g_ref[...]

        def cast_rows(r, _):
            sl = pl.ds(pl.multiple_of(r * rows_per_step, rows_per_step), rows_per_step)
            hn_ref[sl, :] = (x_ref[sl, :] * g).astype(BF16)
            return 0

        lax.fori_loop(0, tm // rows_per_step, cast_rows, 0)

    def project(head_norm, first):
        if first:
            for r in range(tm // rows_per_step):
                rows = slice(r * rows_per_step, (r + 1) * rows_per_step)
                r_ref[rows, :] = jnp.broadcast_to(_rms_scale(x_ref[rows, :]),
                                                  (rows_per_step, LANES))
        for d in range(tn // dot_cols):
            acc = jnp.dot(hn_ref[...], w_ref[:, d * dot_cols:(d + 1) * dot_cols],
                          preferred_element_type=F32)
            for c in range(dot_cols // LANES):
                col = d * (dot_cols // LANES) + c
                a = acc[:, c * LANES:(c + 1) * LANES] * r_ref[...]
                if head_norm:
                    a = a * _rms_scale(a) * hg_ref[:, col * LANES:(col + 1) * LANES]
                o_ref[col] = a.astype(BF16)

    @pl.when(j == 0)
    def _():
        project(True, True)

    @pl.when(jnp.logical_and(j > 0, j < head_norm_tiles))
    def _():
        project(True, False)

    @pl.when(j >= head_norm_tiles)
    def _():
        project(False, False)


def _inproj(x2d, g, w_bf16, layer, head_gain, *, tm, tn, dot_cols):
    m, d = x2d.shape
    n = w_bf16.shape[2]
    head_norm_tiles = head_gain.shape[1] // tn
    assert head_norm_tiles * tn == head_gain.shape[1]
    kern = functools.partial(_inproj_kernel, tm=tm, tn=tn, rows_per_step=128,
                             dot_cols=dot_cols, head_norm_tiles=head_norm_tiles)
    return pl.pallas_call(
        kern,
        out_shape=jax.ShapeDtypeStruct((n // LANES, m, LANES), BF16),
        grid=(m // tm, n // tn),
        in_specs=[
            pl.BlockSpec((tm, d), lambda i, j: (i, 0)),
            pl.BlockSpec((1, d), lambda i, j: (0, 0)),
            pl.BlockSpec((None, d, tn), lambda i, j: (layer, 0, j)),
            pl.BlockSpec((1, tn), lambda i, j: (0, jnp.minimum(j, head_norm_tiles - 1))),
        ],
        out_specs=pl.BlockSpec((tn // LANES, tm, LANES), lambda i, j: (j, i, 0)),
        scratch_shapes=[pltpu.VMEM((tm, d), BF16), pltpu.VMEM((tm, LANES), F32)],
        compiler_params=pltpu.CompilerParams(
            dimension_semantics=("parallel", "arbitrary"),
            vmem_limit_bytes=VMEM_LIMIT_BYTES),
        name="inproj",
    )(x2d, g.reshape(1, d), w_bf16, head_gain)


def _cast_jobs(casts, n_steps, flat_step):
    in_specs, out_specs, out_shapes = [], [], []
    for w_f32, first_layer, n_layers in casts:
        _, rows, cols = w_f32.shape
        rb = n_layers * rows // n_steps
        assert rb * n_steps == n_layers * rows and rb % 16 == 0 and rows % rb == 0
        per_layer = rows // rb

        def block_index(*grid_indices, first=0, per_layer=per_layer):
            step = flat_step(*grid_indices)
            return (first + step // per_layer, step % per_layer, 0)

        in_specs.append(pl.BlockSpec((None, rb, cols),
                                     functools.partial(block_index, first=first_layer)))
        out_specs.append(pl.BlockSpec((None, rb, cols), block_index))
        out_shapes.append(jax.ShapeDtypeStruct((n_layers, rows, cols), BF16))
    return in_specs, out_specs, out_shapes


def _attn_kernel(q_ref, k_ref, v_ref, gate_ref, *refs, tq, nsub, n_casts):
    cast_src = refs[:n_casts]
    o_ref = refs[n_casts]
    cast_dst = refs[n_casts + 1:2 * n_casts + 1]
    tri_ref = refs[2 * n_casts + 1]
    qi = pl.program_id(2)
    tk = tq
    first = qi * nsub

    @pl.when(qi == 0)
    def _():
        r = lax.broadcasted_iota(jnp.int32, (tk, tk), 0)
        c = lax.broadcasted_iota(jnp.int32, (tk, tk), 1)
        tri_ref[...] = jnp.where(r > c, 1.0, 0.0).astype(BF16)

    for src, dst in zip(cast_src, cast_dst):
        dst[...] = src[...].astype(BF16)

    tri = tri_ref[...]
    row = lax.broadcasted_iota(jnp.int32, (tq, tk), 0)
    col = lax.broadcasted_iota(jnp.int32, (tq, tk), 1)
    causal = col < row

    def key_slice(kb):
        return pl.ds(pl.multiple_of(jnp.maximum(kb, 0) * tk, tk), tk)

    def exists(kb):
        return (kb >= 0).astype(F32).astype(BF16)

    def logits(q, kb):
        return lax.dot_general(q, k_ref[key_slice(kb), :], (((1,), (1,)), ((), ())),
                               preferred_element_type=F32)

    def sticks(z, mask):
        log_beta = jnp.minimum(z, 0.0) - jnp.log(1.0 + jnp.exp2(-jnp.abs(z))) * LOG2E
        log_1m = log_beta - z
        if mask is not None:
            log_1m = jnp.where(mask, log_1m, 0.0)
        return log_beta, log_1m

    def suffix_sums(log_1m):
        return jnp.dot(log_1m.astype(BF16), tri, preferred_element_type=F32)

    def weights(log_beta, suffix, carry, mask):
        attn = jnp.exp2(log_beta + suffix if carry is None else log_beta + suffix + carry)
        if mask is not None:
            attn = jnp.where(mask, attn, 0.0)
        return attn.astype(BF16)

    def values(kb, scale):
        v = v_ref[key_slice(kb), :]
        return v if scale is None else v * scale

    n = nsub
    q_rows = [slice(max(j - 1, 0) * tq, min(j + 1, n) * tq) for j in range(n + 1)]
    z, log_beta, log_1m, suffix = {}, {}, {}, {}
    carries = [None] * n
    accs = [None] * n

    def stage_logits(j):
        z[j] = logits(q_ref[q_rows[j], :], first - 1 + j)

    def stage_sticks(j):
        parts = []
        if j >= 1:
            parts.append(sticks(z[j][:tq], causal))
        if j <= n - 1:
            parts.append(sticks(z[j][-tq:], None))
        log_beta[j] = [p[0] for p in parts]
        log_1m[j] = [p[1] for p in parts]

    def stage_suffix(j):
        both = len(log_1m[j]) == 2
        s = suffix_sums(jnp.concatenate(log_1m[j], axis=0) if both else log_1m[j][0])
        suffix[j] = [s[:tq], s[tq:]] if both else [s]

    def stage_values(j):
        attn = []
        idx = 0
        if j >= 1:
            s = j - 1
            attn.append(weights(log_beta[j][idx], suffix[j][idx], None, causal))
            carries[s] = suffix[j][idx][:, :1] + log_1m[j][idx][:, :1]
            idx += 1
        if j <= n - 1:
            s = j
            attn.append(weights(log_beta[j][idx], suffix[j][idx], carries[s], None))
            carries[s] = carries[s] + (suffix[j][idx][:, :1] + log_1m[j][idx][:, :1])
        v = values(first - 1 + j, exists(first - 1) if j == 0 else None)
        pv = jnp.dot(attn[0] if len(attn) == 1 else jnp.concatenate(attn, axis=0), v,
                     preferred_element_type=F32)
        idx = 0
        if j >= 1:
            accs[j - 1] = pv[:tq]
            idx = 1
        if j <= n - 1:
            part = pv[idx * tq:(idx + 1) * tq]
            accs[j] = part if accs[j] is None else accs[j] + part

    stages = ((stage_logits, 0), (stage_sticks, 1), (stage_suffix, 1), (stage_values, 2))
    for t in range(n + 1 + stages[-1][1]):
        for stage, lag in stages:
            j = n - (t - lag)
            if 0 <= j <= n:
                stage(j)

    def live(state):
        it, carries, _ = state
        alive = jnp.bool_(False)
        for s in range(n):
            more = jnp.logical_and(first + s - 2 - it >= 0,
                                   jnp.max(carries[s]) > EXP2_UNDERFLOW)
            alive = jnp.logical_or(alive, more)
        return alive

    def sweep(state):
        it, carries, accs = state
        out_c, out_a = [], []
        for s in range(n):
            kb = first + s - 2 - it
            lb, l1m = sticks(logits(q_ref[s * tq:(s + 1) * tq, :], kb), None)
            suf = suffix_sums(l1m)
            attn = weights(lb, suf, carries[s], None)
            out_a.append(accs[s] + jnp.dot(attn, values(kb, exists(kb)),
                                           preferred_element_type=F32))
            out_c.append(carries[s] + (suf[:, :1] + l1m[:, :1]))
        return it + 1, tuple(out_c), tuple(out_a)

    _, _, accs = lax.while_loop(live, sweep, (jnp.int32(0), tuple(carries), tuple(accs)))
    for s in range(n):
        rows = slice(s * tq, (s + 1) * tq)
        o_ref[rows, :] = (accs[s] * _silu(gate_ref[rows, :].astype(F32))).astype(BF16)


def _attention(proj4, casts, *, heads, tq, nsub):
    _, b, s, _ = proj4.shape
    rows = tq * nsub
    steps = s // rows
    cast_in, cast_out, cast_shapes = _cast_jobs(
        casts, b * heads * steps, lambda bi, h, qi: (bi * heads + h) * steps + qi)
    kern = functools.partial(_attn_kernel, tq=tq, nsub=nsub, n_casts=len(casts))
    blk = lambda off: pl.BlockSpec((None, None, rows, LANES),
                                   lambda bi, h, qi: (off + h, bi, qi, 0))
    full = lambda off: pl.BlockSpec((None, None, s, LANES),
                                    lambda bi, h, qi: (off + h, bi, 0, 0))
    outs = pl.pallas_call(
        kern,
        out_shape=[jax.ShapeDtypeStruct((heads, b, s, LANES), BF16)] + cast_shapes,
        grid=(b, heads, steps),
        in_specs=[blk(0), full(heads), full(2 * heads), blk(3 * heads)] + cast_in,
        out_specs=[pl.BlockSpec((None, None, rows, LANES),
                                lambda bi, h, qi: (h, bi, qi, 0))] + cast_out,
        scratch_shapes=[pltpu.VMEM((tq, tq), BF16)],
        compiler_params=pltpu.CompilerParams(
            dimension_semantics=("parallel", "parallel", "arbitrary"),
            vmem_limit_bytes=VMEM_LIMIT_BYTES),
        name="stickbreak_attn",
    )(proj4, proj4, proj4, proj4, *[c[0] for c in casts])
    return outs[0], outs[1:]


def _sgu_kernel(u_ref, v_ref, gate_ref, g_ref, w_ref, b_ref, o_ref, *, heads, ts):
    row = lax.broadcasted_iota(jnp.int32, (CHUNK, CHUNK), 0)
    col = lax.broadcasted_iota(jnp.int32, (CHUNK, CHUNK), 1)
    for h in range(heads):
        w = jnp.where(col <= row, w_ref[h], 0.0).astype(BF16)
        bias = b_ref[h]
        g = g_ref[h]
        for c in range(ts // CHUNK):
            sl = slice(c * CHUNK, (c + 1) * CHUNK)
            v = v_ref[h, sl, :].astype(F32)
            vn = (v * _rms_scale(v) * g).astype(BF16)
            mixed = jnp.dot(w, vn, preferred_element_type=F32) + bias
            y = u_ref[h, sl, :].astype(F32) * mixed * _silu(gate_ref[h, sl, :].astype(F32))
            o_ref[h, sl, :] = y.astype(BF16)


def _sgu(proj3, sgu_g, w_s, b_s, *, heads, first_block, ts):
    _, m, _ = proj3.shape
    kern = functools.partial(_sgu_kernel, heads=heads, ts=ts)
    base = first_block // heads
    blk = lambda k: pl.BlockSpec((heads, ts, LANES), lambda i: (base + k, i, 0))
    b_bcast = jnp.broadcast_to(b_s[:, :, None], (heads, CHUNK, LANES))
    return pl.pallas_call(
        kern,
        out_shape=jax.ShapeDtypeStruct((heads, m, LANES), BF16),
        grid=(m // ts,),
        in_specs=[blk(0), blk(1), blk(2),
                  pl.BlockSpec((heads, 1, LANES), lambda i: (0, 0, 0)),
                  pl.BlockSpec((heads, CHUNK, CHUNK), lambda i: (0, 0, 0)),
                  pl.BlockSpec((heads, CHUNK, LANES), lambda i: (0, 0, 0))],
        out_specs=pl.BlockSpec((heads, ts, LANES), lambda i: (0, i, 0)),
        compiler_params=pltpu.CompilerParams(
            dimension_semantics=("parallel",),
            vmem_limit_bytes=VMEM_LIMIT_BYTES),
        name="spatial_gating",
    )(proj3, proj3, proj3, sgu_g.reshape(heads, 1, LANES), w_s, b_bcast)


def _pool_kernel(x_ref, halo_ref, gate_ref, w_ref, scale_ref, o_ref, band_ref, band_h_ref, *,
                 ts, halo):
    i = pl.program_id(1)

    @pl.when(jnp.logical_and(pl.program_id(0) == 0, i == 0))
    def _():
        t = lax.broadcasted_iota(jnp.int32, (ts, ts), 0)
        j = lax.broadcasted_iota(jnp.int32, (ts, ts), 1)
        th = lax.broadcasted_iota(jnp.int32, (ts, halo), 0)
        jh = lax.broadcasted_iota(jnp.int32, (ts, halo), 1)
        for g, window in enumerate(POOL_WINDOWS):
            band_ref[g] = jnp.where((j <= t) & (j > t - window), 1.0, 0.0).astype(BF16)
            band_h_ref[g] = jnp.where(jh - halo > th - window, 1.0, 0.0).astype(BF16)

    pos = i * ts + lax.broadcasted_iota(jnp.int32, (ts, 1), 0)
    has_prev = (i > 0).astype(F32).astype(BF16)
    groups = range(len(POOL_WINDOWS))
    x = [jnp.concatenate([x_ref[2 * g], x_ref[2 * g + 1]], axis=-1) for g in groups]
    xh = [jnp.concatenate([halo_ref[2 * g], halo_ref[2 * g + 1]], axis=-1) * has_prev
          for g in groups]
    wsum = [jnp.dot(band_ref[g], x[g], preferred_element_type=F32)
            + jnp.dot(band_h_ref[g], xh[g], preferred_element_type=F32) for g in groups]
    pooled = [wsum[g] / jnp.minimum(pos + 1, window).astype(F32) - x[g].astype(F32)
              for g, window in enumerate(POOL_WINDOWS)]
    mapped = [jnp.dot(pooled[g].astype(BF16), w_ref[g], preferred_element_type=F32)
              for g in groups]
    for g in groups:
        for half in range(2):
            c = 2 * g + half
            lanes = slice(half * LANES, (half + 1) * LANES)
            y = mapped[g][:, lanes] * scale_ref[c] * _silu(gate_ref[c].astype(F32))
            o_ref[c] = y.astype(BF16)


def _pool(proj3, w_pool_bf16, layer, pool_scale, *, batch, seq, first_block, ts):
    _, m, _ = proj3.shape
    nblk = 2 * len(POOL_WINDOWS)
    halo = LANES
    assert max(POOL_WINDOWS) <= halo and ts % halo == 0
    kern = functools.partial(_pool_kernel, ts=ts, halo=halo)
    base = first_block // nblk
    steps = seq // ts
    blk = lambda k: pl.BlockSpec((nblk, ts, LANES),
                                 lambda b, i: (base + k, b * steps + i, 0))
    halo_spec = pl.BlockSpec(
        (nblk, halo, LANES),
        lambda b, i: (base, jnp.maximum((b * seq + i * ts) // halo - 1, 0), 0))
    return pl.pallas_call(
        kern,
        out_shape=jax.ShapeDtypeStruct((nblk, m, LANES), BF16),
        grid=(batch, steps),
        in_specs=[blk(0), halo_spec, blk(1),
                  pl.BlockSpec((None,) + w_pool_bf16.shape[1:], lambda b, i: (layer, 0, 0, 0)),
                  pl.BlockSpec((nblk, 1, LANES), lambda b, i: (0, 0, 0))],
        out_specs=pl.BlockSpec((nblk, ts, LANES), lambda b, i: (0, b * steps + i, 0)),
        scratch_shapes=[pltpu.VMEM((len(POOL_WINDOWS), ts, ts), BF16),
                        pltpu.VMEM((len(POOL_WINDOWS), ts, halo), BF16)],
        compiler_params=pltpu.CompilerParams(
            dimension_semantics=("arbitrary", "arbitrary"),
            vmem_limit_bytes=VMEM_LIMIT_BYTES),
        name="causal_pool",
    )(proj3, proj3, proj3, w_pool_bf16, pool_scale.reshape(nblk, 1, LANES))


def _outproj_kernel(ya_ref, yb_ref, yc_ref, w_ref, x_ref, o_ref, y_ref, *, tn, dot_cols):
    j = pl.program_id(1)

    @pl.when(j == 0)
    def _():
        c = 0
        for ref in (ya_ref, yb_ref, yc_ref):
            for k in range(ref.shape[0]):
                y_ref[:, c * LANES:(c + 1) * LANES] = ref[k]
                c += 1

    for d in range(tn // dot_cols):
        cols = slice(d * dot_cols, (d + 1) * dot_cols)
        o_ref[:, cols] = x_ref[:, cols] + jnp.dot(y_ref[...], w_ref[:, cols],
                                                  preferred_element_type=F32)


def _outproj(ya, yb, yc, w_bf16, layer, x2d, *, tm, tn):
    m, d = x2d.shape
    kdim = w_bf16.shape[1]
    yspec = lambda a: pl.BlockSpec((a.shape[0], tm, LANES), lambda i, j: (0, i, 0))
    return pl.pallas_call(
        functools.partial(_outproj_kernel, tn=tn, dot_cols=2 * LANES),
        out_shape=jax.ShapeDtypeStruct((m, d), F32),
        grid=(m // tm, d // tn),
        in_specs=[yspec(ya), yspec(yb), yspec(yc),
                  pl.BlockSpec((None, kdim, tn), lambda i, j: (layer, 0, j)),
                  pl.BlockSpec((tm, tn), lambda i, j: (i, j))],
        out_specs=pl.BlockSpec((tm, tn), lambda i, j: (i, j)),
        scratch_shapes=[pltpu.VMEM((tm, kdim), BF16)],
        compiler_params=pltpu.CompilerParams(
            dimension_semantics=("parallel", "arbitrary"),
            vmem_limit_bytes=VMEM_LIMIT_BYTES),
        name="outproj",
    )(ya, yb, yc, w_bf16, x2d)


def kernel(x, norm_g, w_in, q_norm_g, k_norm_g, sgu_norm_g, w_spatial, b_spatial,
           w_pool, pool_scale, w_out):
    batch, seq, d_model = x.shape
    depth = norm_g.shape[0]
    b_heads = w_spatial.shape[1]
    b_width = b_heads * HEAD_DIM
    c_width = pool_scale.shape[1]
    a_width = d_model - b_width - c_width
    a_heads = a_width // HEAD_DIM
    in_cols = w_in.shape[2]
    assert in_cols == 4 * a_width + 3 * b_width + 2 * c_width
    assert w_pool.shape[1:] == (len(POOL_WINDOWS), 2 * LANES, 2 * LANES)
    m = batch * seq

    w_in_bf16 = w_in[:1].astype(BF16)
    w_pool_bf16 = w_pool.astype(BF16)
    w_out_bf16 = None

    x2d = x.reshape(m, d_model)
    for layer in range(depth):
        head_gain = jnp.concatenate([jnp.tile(q_norm_g[layer] * (LOG2E / math.sqrt(HEAD_DIM)), a_heads),
                                     jnp.tile(k_norm_g[layer], a_heads)]).reshape(1, 2 * a_width)
        proj3 = _inproj(x2d, norm_g[layer], w_in_bf16, 0, head_gain, **TILES["inproj"])
        proj4 = proj3.reshape(in_cols // LANES, batch, seq, LANES)
        casts = []
        if layer + 1 < depth:
            casts.append((w_in, layer + 1, 1))
        if layer == 0:
            casts.append((w_out, 0, depth))
        ya, casted = _attention(proj4, casts, heads=a_heads, **TILES["attention"])
        if layer + 1 < depth:
            w_in_bf16 = casted[0]
        if layer == 0:
            w_out_bf16 = casted[-1]
        yb = _sgu(proj3, sgu_norm_g[layer], w_spatial[layer], b_spatial[layer],
                  heads=b_heads, first_block=4 * a_heads, **TILES["sgu"])
        yc = _pool(proj3, w_pool_bf16, layer, pool_scale[layer], batch=batch, seq=seq,
                   first_block=4 * a_heads + 3 * b_heads, **TILES["pool"])
        x2d = _outproj(ya.reshape(a_heads, m, LANES), yb, yc, w_out_bf16, layer, x2d,
                       **TILES["outproj"])
    return x2d.reshape(batch, seq, d_model)
```
